```python
import jax
import jax.numpy as jnp
from jax import lax
import numpy as np

D_MODEL = 1024
BATCH = 4
SEQ = 8192
DEPTH = 2
DEC_BATCH = 16
DEC_SEQ = 64
PAST_LEN = 4096

CHUNK = 64
D_CONV = D_MODEL // 4
D_GMLP = D_MODEL // 4
D_ATT = D_MODEL // 2
D_MIX = D_CONV + D_GMLP + D_ATT
CONV_WIDTH = 31
CONV_PAD = CONV_WIDTH - 1
GM_HEAD_DIM = 64
GM_HEADS = D_GMLP // GM_HEAD_DIM
GM_CHUNK = 128
HEAD_DIM = 64
ATT_HEADS = D_ATT // HEAD_DIM
BAND_CHUNKS = 8
BAND_PAST = BAND_CHUNKS * CHUNK
BAND_LEN = BAND_PAST + CHUNK
REL_CLIP = 128
N_REL = 2 * REL_CLIP + 1
D_IN = 2 * D_CONV + 2 * D_GMLP + 3 * D_ATT
D_FF = ((8 * D_MODEL // 3 + 127) // 128) * 128
EPS = 1e-6

kernel_name = 'hybrid_streaming_encoder_step'


def rms_norm(x, g):
    xf = x.astype(jnp.float32)
    y = xf * lax.rsqrt(jnp.mean(xf * xf, axis=-1, keepdims=True) + EPS)
    return (y * g.astype(jnp.float32)).astype(x.dtype)


def layer_norm(x, g, b):
    xf = x.astype(jnp.float32)
    mu = jnp.mean(xf, axis=-1, keepdims=True)
    var = jnp.mean(jnp.square(xf - mu), axis=-1, keepdims=True)
    y = (xf - mu) * lax.rsqrt(var + EPS)
    return (y * g.astype(jnp.float32) + b.astype(jnp.float32)).astype(x.dtype)


def swiglu_ffn(x, g, w_gate, w_up, w_down):
    h = rms_norm(x, g)
    return (jax.nn.silu(h @ w_gate) * (h @ w_up)) @ w_down


def begin_mixing(x, p):
    B, T, _ = x.shape
    x = x + 0.5 * swiglu_ffn(x, p['ffn1_norm'], p['ffn1_w_gate'], p['ffn1_w_up'], p['ffn1_w_down'])
    z = rms_norm(x, p['mix_norm']) @ p['w_in']
    o1 = 2 * D_CONV
    o2 = o1 + 2 * D_GMLP
    o3 = o2 + D_ATT
    o4 = o3 + D_ATT
    zc, zg = z[..., :o1], z[..., o1:o2]
    zq, zk, zv = z[..., o2:o3], z[..., o3:o4], z[..., o4:]
    glu = zc[..., :D_CONV] * jax.nn.sigmoid(zc[..., D_CONV:])
    uv = jax.nn.gelu(zg)
    u = uv[..., :D_GMLP].reshape(B, T, GM_HEADS, GM_HEAD_DIM)
    vg = layer_norm(uv[..., D_GMLP:], p['gmlp_ln_g'], p['gmlp_ln_b']).reshape(B, T, GM_HEADS, GM_HEAD_DIM)
    q = rms_norm(zq.reshape(B, T, ATT_HEADS, HEAD_DIM), p['q_norm'])
    k = rms_norm(zk.reshape(B, T, ATT_HEADS, HEAD_DIM), p['k_norm'])
    v = zv.reshape(B, T, ATT_HEADS, HEAD_DIM)
    return x, glu, u, vg, q, k, v


def conv_branch(glu_padded, p):
    y = lax.conv_general_dilated(glu_padded, p['conv_w'][:, None, :], (1,), 'VALID',
                                 dimension_numbers=('NWC', 'WIO', 'NWC'),
                                 feature_group_count=D_CONV) + p['conv_b']
    return jax.nn.silu(layer_norm(y, p['conv_ln_g'], p['conv_ln_b']))


def gmlp_spatial(v, ws, bs):
    tc = v.shape[2]
    w = jnp.tril(ws[:, :tc, :tc])
    return jnp.einsum('hts,bnshd->bnthd', w, v) + bs[:, :tc].T[None, None, :, :, None]


def band_attention(q, k, v, q_pos, k_pos, rel_bias):
    rel = q_pos[:, None] - k_pos[None, :]
    bias = rel_bias[:, jnp.clip(rel, -REL_CLIP, REL_CLIP) + REL_CLIP].astype(jnp.float32)
    qc = q_pos[:, None] // CHUNK
    kc = k_pos[None, :] // CHUNK
    allowed = (k_pos[None, :] >= 0) & (kc <= qc) & (kc >= qc - BAND_CHUNKS)
    s = jnp.einsum('bqhd,bkhd->bhqk', q, k).astype(jnp.float32) * (HEAD_DIM ** -0.5) + bias
    s = jnp.where(allowed, s, -1e30)
    pr = jax.nn.softmax(s, axis=-1)
    return jnp.einsum('bhqk,bkhd->bqhd', pr.astype(v.dtype), v)


def prompt_band_attention(q, k, v, rel_bias):
    B, S, H, Dh = q.shape
    n_chunks = S // CHUNK
    pad = ((0, 0), (BAND_PAST, 0), (0, 0), (0, 0))
    kp = jnp.pad(k, pad)
    vp = jnp.pad(v, pad)

    def one_chunk(c):
        start = c * CHUNK
        qs = lax.dynamic_slice_in_dim(q, start, CHUNK, axis=1)
        ks = lax.dynamic_slice_in_dim(kp, start, BAND_LEN, axis=1)
        vs = lax.dynamic_slice_in_dim(vp, start, BAND_LEN, axis=1)
        q_pos = start + jnp.arange(CHUNK)
        k_pos = start - BAND_PAST + jnp.arange(BAND_LEN)
        return band_attention(qs, ks, vs, q_pos, k_pos, rel_bias)

    out = lax.map(one_chunk, jnp.arange(n_chunks))
    return out.transpose(1, 0, 2, 3, 4).reshape(B, S, H, Dh)


def finish_layer(x, conv_out, gm_out, att_out, p):
    B, T, _ = x.shape
    mixed = jnp.concatenate([conv_out, gm_out.reshape(B, T, D_GMLP), att_out.reshape(B, T, D_ATT)], axis=-1)
    x = x + mixed @ p['w_out']
    return x + 0.5 * swiglu_ffn(x, p['ffn2_norm'], p['ffn2_w_gate'], p['ffn2_w_up'], p['ffn2_w_down'])


def prompt_layer(x, p):
    B, S, _ = x.shape
    x, glu, u, vg, q, k, v = begin_mixing(x, p)
    conv_out = conv_branch(jnp.pad(glu, ((0, 0), (CONV_PAD, 0), (0, 0))), p)
    s = gmlp_spatial(vg.reshape(B, S // GM_CHUNK, GM_CHUNK, GM_HEADS, GM_HEAD_DIM), p['gmlp_ws'], p['gmlp_b'])
    gm_out = u * s.reshape(B, S, GM_HEADS, GM_HEAD_DIM)
    att_out = prompt_band_attention(q, k, v, p['rel_bias'])
    x = finish_layer(x, conv_out, gm_out, att_out, p)
    keep = min(BAND_PAST, S)
    return x, glu[:, -CONV_PAD:], k[:, -keep:], v[:, -keep:]


def sample_layer(x, conv_buf, k_cache, v_cache, p):
    B, T, _ = x.shape
    L = k_cache.shape[1]
    x, glu, u, vg, q, k, v = begin_mixing(x, p)
    conv_in = jnp.concatenate([conv_buf.astype(glu.dtype), glu], axis=1)
    conv_out = conv_branch(conv_in, p)
    gm_out = u * gmlp_spatial(vg[:, None], p['gmlp_ws'], p['gmlp_b'])[:, 0]
    q_pos = PAST_LEN + jnp.arange(T)
    k_pos = PAST_LEN + jnp.arange(-L, T)
    k_all = jnp.concatenate([k_cache.astype(k.dtype), k], axis=1)
    v_all = jnp.concatenate([v_cache.astype(v.dtype), v], axis=1)
    att_out = band_attention(q, k_all, v_all, q_pos, k_pos, p['rel_bias'])
    x = finish_layer(x, conv_out, gm_out, att_out, p)
    return x, conv_in[:, -CONV_PAD:], k, v, vg


def setup_inputs(seed: int = 0) -> dict:
    key = jax.random.key(seed)
    ks = jax.random.split(key, 32)
    att_cache = min(BAND_PAST, PAST_LEN)

    def nrm(k, shape, scale):
        return jax.random.normal(k, shape, jnp.float32) * scale

    return {
        'x_prompt': nrm(ks[0], (BATCH, SEQ, D_MODEL), 1.0),
        'x_sample': nrm(ks[1], (DEC_BATCH, DEC_SEQ, D_MODEL), 1.0),
        'cache_conv': nrm(ks[2], (DEPTH, DEC_BATCH, CONV_PAD, D_CONV), 0.5),
        'cache_k': nrm(ks[3], (DEPTH, DEC_BATCH, att_cache, ATT_HEADS, HEAD_DIM), 1.0),
        'cache_v': nrm(ks[4], (DEPTH, DEC_BATCH, att_cache, ATT_HEADS, HEAD_DIM), 1.0),
        'ffn1_norm': 1.0 + nrm(ks[5], (DEPTH, D_MODEL), 0.01),
        'ffn1_w_gate': nrm(ks[6], (DEPTH, D_MODEL, D_FF), D_MODEL ** -0.5),
        'ffn1_w_up': nrm(ks[7], (DEPTH, D_MODEL, D_FF), D_MODEL ** -0.5),
        'ffn1_w_down': nrm(ks[8], (DEPTH, D_FF, D_MODEL), D_FF ** -0.5),
        'mix_norm': 1.0 + nrm(ks[9], (DEPTH, D_MODEL), 0.01),
        'w_in': nrm(ks[10], (DEPTH, D_MODEL, D_IN), D_MODEL ** -0.5),
        'conv_w': nrm(ks[11], (DEPTH, CONV_WIDTH, D_CONV), CONV_WIDTH ** -0.5),
        'conv_b': nrm(ks[12], (DEPTH, D_CONV), 0.01),
        'conv_ln_g': 1.0 + nrm(ks[13], (DEPTH, D_CONV), 0.01),
        'conv_ln_b': nrm(ks[14], (DEPTH, D_CONV), 0.01),
        'gmlp_ln_g': 1.0 + nrm(ks[15], (DEPTH, D_GMLP), 0.01),
        'gmlp_ln_b': nrm(ks[16], (DEPTH, D_GMLP), 0.01),
        'gmlp_ws': nrm(ks[17], (DEPTH, GM_HEADS, GM_CHUNK, GM_CHUNK), GM_CHUNK ** -0.5),
        'gmlp_b': 1.0 + nrm(ks[18], (DEPTH, GM_HEADS, GM_CHUNK), 0.1),
        'q_norm': 1.0 + nrm(ks[19], (DEPTH, HEAD_DIM), 0.01),
        'k_norm': 1.0 + nrm(ks[20], (DEPTH, HEAD_DIM), 0.01),
        'rel_bias': nrm(ks[21], (DEPTH, ATT_HEADS, N_REL), 0.1),
        'w_out': nrm(ks[22], (DEPTH, D_MIX, D_MODEL), D_MIX ** -0.5),
        'ffn2_norm': 1.0 + nrm(ks[23], (DEPTH, D_MODEL), 0.01),
        'ffn2_w_gate': nrm(ks[24], (DEPTH, D_MODEL, D_FF), D_MODEL ** -0.5),
        'ffn2_w_up': nrm(ks[25], (DEPTH, D_MODEL, D_FF), D_MODEL ** -0.5),
        'ffn2_w_down': nrm(ks[26], (DEPTH, D_FF, D_MODEL), D_FF ** -0.5),
    }


def reference(x_prompt, x_sample, cache_conv, cache_k, cache_v,
              ffn1_norm, ffn1_w_gate, ffn1_w_up, ffn1_w_down,
              mix_norm, w_in, conv_w, conv_b, conv_ln_g, conv_ln_b,
              gmlp_ln_g, gmlp_ln_b, gmlp_ws, gmlp_b,
              q_norm, k_norm, rel_bias, w_out,
              ffn2_norm, ffn2_w_gate, ffn2_w_up, ffn2_w_down):
    xp = x_prompt
    xs = x_sample
    p_conv, p_k, p_v = [], [], []
    s_conv, s_k, s_v, s_gv = [], [], [], []
    for l in range(DEPTH):
        p = {
            'ffn1_norm': ffn1_norm[l], 'ffn1_w_gate': ffn1_w_gate[l], 'ffn1_w_up': ffn1_w_up[l],
            'ffn1_w_down': ffn1_w_down[l], 'mix_norm': mix_norm[l], 'w_in': w_in[l],
            'conv_w': conv_w[l], 'conv_b': conv_b[l], 'conv_ln_g': conv_ln_g[l], 'conv_ln_b': conv_ln_b[l],
            'gmlp_ln_g': gmlp_ln_g[l], 'gmlp_ln_b': gmlp_ln_b[l], 'gmlp_ws': gmlp_ws[l], 'gmlp_b': gmlp_b[l],
            'q_norm': q_norm[l], 'k_norm': k_norm[l], 'rel_bias': rel_bias[l], 'w_out': w_out[l],
            'ffn2_norm': ffn2_norm[l], 'ffn2_w_gate': ffn2_w_gate[l], 'ffn2_w_up': ffn2_w_up[l],
            'ffn2_w_down': ffn2_w_down[l],
        }
        xp, pc, pk, pv = prompt_layer(xp, p)
        xs, sc, sk, sv, sg = sample_layer(xs, cache_conv[l], cache_k[l], cache_v[l], p)
        p_conv.append(pc)
        p_k.append(pk)
        p_v.append(pv)
        s_conv.append(sc)
        s_k.append(sk)
        s_v.append(sv)
        s_gv.append(sg)
    prompt_conv_state = jnp.stack(p_conv)
    prompt_k = jnp.stack(p_k)
    prompt_v = jnp.stack(p_v)
    sample_conv_state = jnp.stack(s_conv)
    sample_k_new = jnp.stack(s_k)
    sample_v_new = jnp.stack(s_v)
    sample_gmlp_v = jnp.stack(s_gv)
    return (xp, xs, prompt_conv_state, prompt_k, prompt_v, sample_conv_state, sample_k_new, sample_v_new, sample_gmlp_v)
```

```python
import functools

import jax
import jax.numpy as jnp
from jax import lax
from jax.experimental import pallas as pl
from jax.experimental.pallas import tpu as pltpu

F32 = jnp.float32
BF16 = jnp.bfloat16

EPS = 1e-6
CHUNK = 64
BAND_CHUNKS = 8
BAND_PAST = BAND_CHUNKS * CHUNK
BAND_LEN = BAND_PAST + CHUNK
REL_CLIP = 128
CONV_WIDTH = 31
CONV_PAD = CONV_WIDTH - 1
CONV_HALO = 32
HEAD_DIM = 64
GM_HEAD_DIM = 64
GM_CHUNK = 128
MASK_VALUE = -1e30

V7X_LANES = 128
V7X_MXU_COLS = 256
V7X_VMEM_BYTES = 64 * 1024 * 1024
HEADS_PER_SLAB = V7X_MXU_COLS // HEAD_DIM

TOKEN_TILE = 512
FF_CHUNK = 512


def _vmem_limit(block_bytes, scratch_bytes, temp_bytes):
    return int(min(V7X_VMEM_BYTES - (4 << 20), block_bytes + scratch_bytes + temp_bytes))


def _nbytes(shape, dtype):
    n = 1
    for s in shape:
        n *= s
    return n * jnp.dtype(dtype).itemsize


def _rms(x, g):
    return x * lax.rsqrt(jnp.mean(x * x, axis=-1, keepdims=True) + EPS) * g


def _layer_norm(x, g, b):
    mu = jnp.mean(x, axis=-1, keepdims=True)
    xc = x - mu
    var = jnp.mean(xc * xc, axis=-1, keepdims=True)
    return xc * lax.rsqrt(var + EPS) * g + b


def _dot(a, b):
    return jnp.dot(a, b, preferred_element_type=F32)


def _swiglu_half_step(x, norm_ref, wg_ref, wu_ref, wd_ref, a_scr):
    d_ff = wg_ref.shape[1]
    h = _rms(x, norm_ref[...]).astype(BF16)
    for c in range(0, d_ff, FF_CHUNK):
        w = min(FF_CHUNK, d_ff - c)
        g = _dot(h, wg_ref[:, c:c + w])
        u = _dot(h, wu_ref[:, c:c + w])
        a_scr[:, c:c + w] = (g * jax.nn.sigmoid(g) * u).astype(BF16)
    return x + 0.5 * _dot(a_scr[...], wd_ref[...])


def _token_kernel(x_ref, n1_ref, wg_ref, wu_ref, wd_ref, nmix_ref, win_ref, lng_ref, lnb_ref,
                  qn_ref, kn_ref, havg_ref,
                  x1_ref, glu_ref, u_ref, vg_ref, q_ref, k_ref, v_ref, ktail_ref, vtail_ref,
                  a_scr, *, tiles_per_seq, d_conv, d_gmlp, d_att):
    x1 = _swiglu_half_step(x_ref[...], n1_ref, wg_ref, wu_ref, wd_ref, a_scr)
    x1_ref[...] = x1
    h = _rms(x1, nmix_ref[...]).astype(BF16)

    o1 = 2 * d_conv
    o2 = o1 + 2 * d_gmlp
    o3 = o2 + d_att
    o4 = o3 + d_att

    zc = _dot(h, win_ref[:, 0:o1])
    glu_ref[...] = zc[:, :d_conv] * jax.nn.sigmoid(zc[:, d_conv:])

    uv = jax.nn.gelu(_dot(h, win_ref[:, o1:o2]))
    u_ref[...] = uv[:, :d_gmlp]
    vg_ref[...] = _layer_norm(uv[:, d_gmlp:], lng_ref[...], lnb_ref[...]).astype(vg_ref.dtype)

    def head_rms(z, g):
        ms = _dot((z * z).astype(BF16), havg_ref[...])
        return z * lax.rsqrt(ms + EPS) * g

    zq = _dot(h, win_ref[:, o2:o3])
    q_ref[...] = head_rms(zq, qn_ref[...] * (HEAD_DIM ** -0.5)).astype(BF16)
    zk = _dot(h, win_ref[:, o3:o4])
    k = head_rms(zk, kn_ref[...])
    k_ref[...] = k.astype(BF16)
    v = _dot(h, win_ref[:, o4:])
    v_ref[...] = v.astype(BF16)

    tail_rows = ktail_ref.shape[0]

    @pl.when(pl.program_id(0) % tiles_per_seq == tiles_per_seq - 1)
    def _():
        ktail_ref[...] = k[k.shape[0] - tail_rows:, :]
        vtail_ref[...] = v[v.shape[0] - tail_rows:, :]


def _resident(shape):
    return pl.BlockSpec(shape, lambda *_: (0,) * len(shape), pipeline_mode=pl.Buffered(1))


def _token_call(x, p, *, seq_len, tail_rows, vg_dtype):
    n_tok, d_model = x.shape
    tm = TOKEN_TILE
    assert n_tok % tm == 0 and (seq_len % tm == 0 or tm % seq_len == 0)
    tiles_per_seq = max(seq_len // tm, 1)
    assert tail_rows <= tm
    n_tail = (n_tok // tm // tiles_per_seq) * tail_rows
    d_ff = p['wg1'].shape[1]
    d_in = p['w_in'].shape[1]
    d_conv, d_gmlp, d_att = p['d_conv'], p['d_gmlp'], p['d_att']

    def rows(width):
        return pl.BlockSpec((tm, width), lambda i: (i, 0))

    tail_spec = pl.BlockSpec((tail_rows, d_att), lambda i: (i // tiles_per_seq, 0))
    in_specs = [
        rows(d_model), _resident((1, d_model)),
        _resident((d_model, d_ff)), _resident((d_model, d_ff)), _resident((d_ff, d_model)),
        _resident((1, d_model)), _resident((d_model, d_in)),
        _resident((1, d_gmlp)), _resident((1, d_gmlp)),
        _resident((1, d_att)), _resident((1, d_att)), _resident((d_att, d_att)),
    ]
    out_shape = [
        jax.ShapeDtypeStruct((n_tok, d_model), F32),
        jax.ShapeDtypeStruct((n_tok, d_conv), F32),
        jax.ShapeDtypeStruct((n_tok, d_gmlp), F32),
        jax.ShapeDtypeStruct((n_tok, d_gmlp), vg_dtype),
        jax.ShapeDtypeStruct((n_tok, d_att), BF16),
        jax.ShapeDtypeStruct((n_tok, d_att), BF16),
        jax.ShapeDtypeStruct((n_tok, d_att), BF16),
        jax.ShapeDtypeStruct((n_tail, d_att), F32),
        jax.ShapeDtypeStruct((n_tail, d_att), F32),
    ]
    out_specs = [rows(d_model), rows(d_conv), rows(d_gmlp), rows(d_gmlp),
                 rows(d_att), rows(d_att), rows(d_att), tail_spec, tail_spec]

    weight_bytes = (3 * _nbytes((d_model, d_ff), BF16) + _nbytes((d_model, d_in), BF16)
                    + _nbytes((d_att, d_att), BF16))
    block_bytes = 2 * (2 * _nbytes((tm, d_model), F32) + 2 * _nbytes((tm, d_conv), F32)
                       + _nbytes((tm, d_gmlp), F32) + 3 * _nbytes((tm, d_att), BF16)
                       + 2 * _nbytes((tail_rows, d_att), F32))
    scratch_bytes = _nbytes((tm, d_ff), BF16)
    temp_bytes = 6 * _nbytes((tm, d_model), F32)

    kernel = functools.partial(_token_kernel, tiles_per_seq=tiles_per_seq,
                               d_conv=d_conv, d_gmlp=d_gmlp, d_att=d_att)
    return pl.pallas_call(
        kernel,
        out_shape=out_shape,
        grid=(n_tok // tm,),
        in_specs=in_specs,
        out_specs=out_specs,
        scratch_shapes=[pltpu.VMEM((tm, d_ff), BF16)],
        compiler_params=pltpu.CompilerParams(
            dimension_semantics=("arbitrary",),
            vmem_limit_bytes=_vmem_limit(weight_bytes + block_bytes, scratch_bytes, temp_bytes)),
        name="token_kernel",
    )(x, p['n1'], p['wg1'], p['wu1'], p['wd1'], p['nmix'], p['w_in'], p['gm_ln_g'], p['gm_ln_b'],
      p['qn'], p['kn'], p['head_avg'])


def _mixer_kernel(x1_ref, gc_ref, gh_ref, u_ref, vg_ref, q_ref, kc_ref, kh_ref, vc_ref, vh_ref,
                  cw_ref, cb_ref, clg_ref, clb_ref, wcat_ref, gmb_ref, bias_ref,
                  wout_ref, n2_ref, wg_ref, wu_ref, wd_ref,
                  y_ref,
                  gbuf, kk, vv, mixed, a_scr, *, first_tile_has_no_past):
    nb_count, tt, d_model = x1_ref.shape
    d_conv = gc_ref.shape[2]
    d_gmlp = u_ref.shape[2]
    d_att = q_ref.shape[2]
    gm_rows = gmb_ref.shape[0]
    t = pl.program_id(1)
    slab = V7X_MXU_COLS
    n_slabs = d_att // slab

    head_of_lane = lax.broadcasted_iota(jnp.int32, (1, slab), 1) // HEAD_DIM
    gm_head_of_lane = lax.broadcasted_iota(jnp.int32, (1, d_gmlp), 1) // GM_HEAD_DIM
    key_index = lax.broadcasted_iota(jnp.int32, (1, BAND_LEN), 1)

    for nb in range(nb_count):
        r0 = nb * tt

        hist = gh_ref[nb]
        if first_tile_has_no_past:
            hist = jnp.where(t > 0, hist, 0.0)
        gbuf[0:CONV_HALO, :] = hist
        gbuf[CONV_HALO:, :] = gc_ref[nb]
        acc = jnp.zeros((tt, d_conv), F32)
        for w in range(CONV_WIDTH):
            acc = acc + gbuf[pl.ds(CONV_HALO - CONV_PAD + w, tt), :] * cw_ref[w:w + 1, :]
        yc = _layer_norm(acc + cb_ref[...], clg_ref[...], clb_ref[...])
        mixed[r0:r0 + tt, 0:d_conv] = (yc * jax.nn.sigmoid(yc)).astype(BF16)

        for ci in range(tt // gm_rows):
            rs = ci * gm_rows
            vgc = vg_ref[nb, rs:rs + gm_rows, :].astype(BF16)
            vstack = jnp.concatenate(
                [jnp.where(gm_head_of_lane == hh, vgc, jnp.zeros_like(vgc))
                 for hh in range(d_gmlp // GM_HEAD_DIM)], axis=0)
            s = _dot(wcat_ref[...], vstack) + gmb_ref[...]
            gm = u_ref[nb, rs:rs + gm_rows, :] * s
            mixed[r0 + rs:r0 + rs + gm_rows, d_conv:d_conv + d_gmlp] = gm.astype(BF16)

        kk[0:BAND_PAST, :] = kh_ref[nb]
        kk[BAND_PAST:, :] = kc_ref[nb]
        vv[0:BAND_PAST, :] = vh_ref[nb]
        vv[BAND_PAST:, :] = vc_ref[nb]
        for c in range(tt // CHUNK):
            cs = c * CHUNK
            for g in range(n_slabs):
                ls = g * slab
                q4 = q_ref[nb, cs:cs + CHUNK, ls:ls + slab]
                qs = jnp.concatenate(
                    [jnp.where(head_of_lane == hl, q4, jnp.zeros_like(q4))
                     for hl in range(HEADS_PER_SLAB)], axis=0)
                kb = kk[cs:cs + BAND_LEN, ls:ls + slab]
                vb = vv[cs:cs + BAND_LEN, ls:ls + slab]
                s = lax.dot_general(qs, kb, (((1,), (1,)), ((), ())), preferred_element_type=F32)
                bias = bias_ref[g]
                if first_tile_has_no_past and c < BAND_CHUNKS:
                    first_valid = jnp.where(t == 0, BAND_PAST - cs, 0)
                    bias = jnp.where(key_index >= first_valid, bias, MASK_VALUE)
                s = s + bias
                m = jnp.max(s, axis=-1, keepdims=True)
                e = jnp.exp(s - m)
                denom = jnp.sum(e, axis=-1, keepdims=True)
                o4 = _dot(e.astype(BF16), vb) * (1.0 / denom)
                o = o4[0:CHUNK, :]
                for hl in range(1, HEADS_PER_SLAB):
                    o = jnp.where(head_of_lane == hl, o4[hl * CHUNK:(hl + 1) * CHUNK, :], o)
                col = d_conv + d_gmlp + ls
                mixed[r0 + cs:r0 + cs + CHUNK, col:col + slab] = o.astype(BF16)

    x1 = x1_ref[...].reshape(nb_count * tt, d_model)
    x2 = x1 + _dot(mixed[...], wout_ref[...])
    y = _swiglu_half_step(x2, n2_ref, wg_ref, wu_ref, wd_ref, a_scr)
    y_ref[...] = y.reshape(nb_count, tt, d_model)


def _mixer_call(x1, glu, glu_hist, u, vg, q, k, k_hist, v, v_hist, p, *, nb, tt, self_history):
    n_seq, seq_len, d_model = x1.shape
    d_conv, d_gmlp, d_att = p['d_conv'], p['d_gmlp'], p['d_att']
    d_ff = p['wg2'].shape[1]
    assert n_seq % nb == 0 and seq_len % tt == 0 and tt % CHUNK == 0
    gm_rows = p['wcat'].shape[0]
    assert tt % gm_rows == 0
    n_t = seq_len // tt
    rows = nb * tt

    def cur(width):
        return pl.BlockSpec((nb, tt, width), lambda b, t: (b, t, 0))

    if self_history:
        assert tt % BAND_PAST == 0 and glu_hist is None and k_hist is None and v_hist is None
        glu_hist, k_hist, v_hist = glu, k, v
        gh_spec = pl.BlockSpec((nb, CONV_HALO, d_conv),
                               lambda b, t: (b, jnp.maximum(t * (tt // CONV_HALO) - 1, 0), 0))
        past_spec = pl.BlockSpec((nb, BAND_PAST, d_att),
                                 lambda b, t: (b, jnp.maximum(t * (tt // BAND_PAST) - 1, 0), 0))
    else:
        assert n_t == 1
        gh_spec = pl.BlockSpec((nb, CONV_HALO, d_conv), lambda b, t: (b, 0, 0))
        past_spec = pl.BlockSpec((nb, BAND_PAST, d_att), lambda b, t: (b, 0, 0))

    n_slabs = d_att // V7X_MXU_COLS
    in_specs = [
        cur(d_model), cur(d_conv), gh_spec, cur(d_gmlp), cur(d_gmlp), cur(d_att),
        cur(d_att), past_spec, cur(d_att), past_spec,
        _resident((CONV_WIDTH, d_conv)), _resident((1, d_conv)), _resident((1, d_conv)), _resident((1, d_conv)),
        _resident((gm_rows, (d_gmlp // GM_HEAD_DIM) * gm_rows)), _resident((gm_rows, d_gmlp)),
        _resident((n_slabs, HEADS_PER_SLAB * CHUNK, BAND_LEN)),
        _resident((d_model, d_model)), _resident((1, d_model)),
        _resident((d_model, d_ff)), _resident((d_model, d_ff)), _resident((d_ff, d_model)),
    ]
    scratch = [
        pltpu.VMEM((CONV_HALO + tt, d_conv), F32),
        pltpu.VMEM((BAND_PAST + tt, d_att), BF16),
        pltpu.VMEM((BAND_PAST + tt, d_att), BF16),
        pltpu.VMEM((rows, d_model), BF16),
        pltpu.VMEM((rows, d_ff), BF16),
    ]
    weight_bytes = (3 * _nbytes((d_model, d_ff), BF16) + _nbytes((d_model, d_model), BF16)
                    + _nbytes((n_slabs, HEADS_PER_SLAB * CHUNK, BAND_LEN), F32)
                    + _nbytes((gm_rows, 4 * gm_rows), BF16) + _nbytes((gm_rows, d_gmlp), F32))
    block_bytes = 2 * (2 * _nbytes((rows, d_model), F32) + 2 * _nbytes((rows, d_conv), F32)
                       + _nbytes((rows, d_gmlp), F32) + 3 * _nbytes((rows, d_att), BF16)
                       + 2 * _nbytes((nb, BAND_PAST, d_att), BF16) + _nbytes((nb, CONV_HALO, d_conv), F32))
    scratch_bytes = (_nbytes((CONV_HALO + tt, d_conv), F32) + 2 * _nbytes((BAND_PAST + tt, d_att), BF16)
                     + _nbytes((rows, d_model), BF16) + _nbytes((rows, d_ff), BF16))
    temp_bytes = 6 * _nbytes((rows, d_model), F32)

    kernel = functools.partial(_mixer_kernel, first_tile_has_no_past=self_history)
    return pl.pallas_call(
        kernel,
        out_shape=jax.ShapeDtypeStruct((n_seq, seq_len, d_model), F32),
        grid=(n_seq // nb, n_t),
        in_specs=in_specs,
        out_specs=cur(d_model),
        scratch_shapes=scratch,
        compiler_params=pltpu.CompilerParams(
            dimension_semantics=("arbitrary", "arbitrary"),
            vmem_limit_bytes=_vmem_limit(weight_bytes + block_bytes, scratch_bytes, temp_bytes)),
        name="mixer_kernel",
    )(x1, glu, glu_hist, u, vg, q, k, k_hist, v, v_hist,
      p['conv_w'], p['conv_b'], p['conv_ln_g'], p['conv_ln_b'], p['wcat'], p['gmb'], p['bias_tab'],
      p['w_out'], p['n2'], p['wg2'], p['wu2'], p['wd2'])


def _row(v):
    return v.reshape(1, -1).astype(F32)


def _gmlp_tables(ws, bs, rows):
    n_heads = ws.shape[0]
    w = jnp.tril(ws[:, :rows, :rows])
    wcat = jnp.transpose(w, (1, 0, 2)).reshape(rows, n_heads * rows).astype(BF16)
    gmb = jnp.repeat(bs[:, :rows].T, GM_HEAD_DIM, axis=1).astype(F32)
    return wcat, gmb


def _band_bias(rel_bias):
    n_heads = rel_bias.shape[0]
    rel = (BAND_PAST + jnp.arange(CHUNK))[:, None] - jnp.arange(BAND_LEN)[None, :]
    tab = rel_bias[:, jnp.clip(rel, -REL_CLIP, REL_CLIP) + REL_CLIP].astype(F32)
    return tab.reshape(n_heads // HEADS_PER_SLAB, HEADS_PER_SLAB * CHUNK, BAND_LEN)


def kernel(x_prompt, x_sample, cache_conv, cache_k, cache_v, ffn1_norm, ffn1_w_gate, ffn1_w_up, ffn1_w_down, mix_norm, w_in, conv_w, conv_b, conv_ln_g, conv_ln_b, gmlp_ln_g, gmlp_ln_b, gmlp_ws, gmlp_b, q_norm, k_norm, rel_bias, w_out, ffn2_norm, ffn2_w_gate, ffn2_w_up, ffn2_w_down):
    batch, seq, d_model = x_prompt.shape
    dec_batch, dec_seq, _ = x_sample.shape
    depth = w_in.shape[0]
    d_conv = conv_w.shape[2]
    d_gmlp = gmlp_ln_g.shape[1]
    n_att_heads = rel_bias.shape[1]
    d_att = n_att_heads * HEAD_DIM
    assert cache_k.shape[2] == BAND_PAST and cache_conv.shape[2] == CONV_PAD
    assert dec_seq == CHUNK and dec_seq <= GM_CHUNK

    wg1, wu1, wd1 = ffn1_w_gate.astype(BF16), ffn1_w_up.astype(BF16), ffn1_w_down.astype(BF16)
    wg2, wu2, wd2 = ffn2_w_gate.astype(BF16), ffn2_w_up.astype(BF16), ffn2_w_down.astype(BF16)
    w_in_b, w_out_b = w_in.astype(BF16), w_out.astype(BF16)
    head_avg = jnp.kron(jnp.eye(n_att_heads, dtype=F32),
                        jnp.full((HEAD_DIM, HEAD_DIM), 1.0 / HEAD_DIM, F32)).astype(BF16)

    xp = x_prompt.reshape(batch * seq, d_model)
    xs = x_sample.reshape(dec_batch * dec_seq, d_model)
    outs = {name: [] for name in ('p_conv', 'p_k', 'p_v', 's_conv', 's_k', 's_v', 's_gv')}
    sample_streams_per_step = TOKEN_TILE // dec_seq // 2

    for l in range(depth):
        p = dict(
            d_conv=d_conv, d_gmlp=d_gmlp, d_att=d_att,
            n1=_row(ffn1_norm[l]), wg1=wg1[l], wu1=wu1[l], wd1=wd1[l],
            nmix=_row(mix_norm[l]), w_in=w_in_b[l],
            gm_ln_g=_row(gmlp_ln_g[l]), gm_ln_b=_row(gmlp_ln_b[l]),
            qn=_row(jnp.tile(q_norm[l], n_att_heads)), kn=_row(jnp.tile(k_norm[l], n_att_heads)),
            head_avg=head_avg,
            conv_w=conv_w[l].astype(F32), conv_b=_row(conv_b[l]),
            conv_ln_g=_row(conv_ln_g[l]), conv_ln_b=_row(conv_ln_b[l]),
            bias_tab=_band_bias(rel_bias[l]),
            w_out=w_out_b[l], n2=_row(ffn2_norm[l]), wg2=wg2[l], wu2=wu2[l], wd2=wd2[l],
        )

        x1, glu, u, vg, q, k, v, k_tail, v_tail = _token_call(
            xp, p, seq_len=seq, tail_rows=min(BAND_PAST, seq), vg_dtype=BF16)
        p['wcat'], p['gmb'] = _gmlp_tables(gmlp_ws[l], gmlp_b[l], GM_CHUNK)
        seq3 = lambda a: a.reshape(batch, seq, a.shape[-1])
        xp = _mixer_call(seq3(x1), seq3(glu), None, seq3(u), seq3(vg), seq3(q), seq3(k), None, seq3(v), None,
                         p, nb=1, tt=TOKEN_TILE, self_history=True).reshape(batch * seq, d_model)
        outs['p_conv'].append(seq3(glu)[:, seq - CONV_PAD:])
        outs['p_k'].append(k_tail.reshape(batch, -1, n_att_heads, HEAD_DIM))
        outs['p_v'].append(v_tail.reshape(batch, -1, n_att_heads, HEAD_DIM))

        x1, glu, u, vg, q, k, v, k_new, v_new = _token_call(
            xs, p, seq_len=dec_seq, tail_rows=TOKEN_TILE, vg_dtype=F32)
        p['wcat'], p['gmb'] = _gmlp_tables(gmlp_ws[l], gmlp_b[l], dec_seq)
        dec3 = lambda a: a.reshape(dec_batch, dec_seq, a.shape[-1])
        conv_hist = jnp.pad(cache_conv[l].astype(F32), ((0, 0), (CONV_HALO - CONV_PAD, 0), (0, 0)))
        k_hist = cache_k[l].reshape(dec_batch, BAND_PAST, d_att).astype(BF16)
        v_hist = cache_v[l].reshape(dec_batch, BAND_PAST, d_att).astype(BF16)
        xs = _mixer_call(dec3(x1), dec3(glu), conv_hist, dec3(u), dec3(vg), dec3(q), dec3(k), k_hist,
                         dec3(v), v_hist, p, nb=sample_streams_per_step, tt=dec_seq,
                         self_history=False).reshape(dec_batch * dec_seq, d_model)
        outs['s_conv'].append(dec3(glu)[:, dec_seq - CONV_PAD:])
        outs['s_k'].append(k_new.reshape(dec_batch, dec_seq, n_att_heads, HEAD_DIM))
        outs['s_v'].append(v_new.reshape(dec_batch, dec_seq, n_att_heads, HEAD_DIM))
        outs['s_gv'].append(vg.reshape(dec_batch, dec_seq, d_gmlp // GM_HEAD_DIM, GM_HEAD_DIM))

    return (xp.reshape(batch, seq, d_model), xs.reshape(dec_batch, dec_seq, d_model),
            jnp.stack(outs['p_conv']), jnp.stack(outs['p_k']), jnp.stack(outs['p_v']),
            jnp.stack(outs['s_conv']), jnp.stack(outs['s_k']), jnp.stack(outs['s_v']),
            jnp.stack(outs['s_gv']))
```

```python
import functools

import jax
import jax.numpy as jnp
from jax import lax
from jax.experimental import pallas as pl
from jax.experimental.pallas import tpu as pltpu

F32 = jnp.float32
BF16 = jnp.bfloat16

EPS = 1e-6
CHUNK = 64
BAND_CHUNKS = 8
BAND_PAST = BAND_CHUNKS * CHUNK
BAND_LEN = BAND_PAST + CHUNK
REL_CLIP = 128
CONV_WIDTH = 31
CONV_PAD = CONV_WIDTH - 1
CONV_HALO = 32
HEAD_DIM = 64
GM_HEAD_DIM = 64
GM_CHUNK = 128
MASK_VALUE = -1e30

V7X_LANES = 128
V7X_SUBLANES = 8
V7X_MXU_COLS = 256
V7X_VMEM_BYTES = 64 * 1024 * 1024
HEADS_PER_SLAB = V7X_MXU_COLS // HEAD_DIM

TOKEN_TILE = 512
FF_CHUNK = 512


def _vmem_limit(block_bytes, scratch_bytes, temp_bytes):
    return int(min(V7X_VMEM_BYTES - (4 << 20), block_bytes + scratch_bytes + temp_bytes))


def _nbytes(shape, dtype):
    n = 1
    for s in shape:
        n *= s
    return n * jnp.dtype(dtype).itemsize


def _rms(x, g):
    return x * lax.rsqrt(jnp.mean(x * x, axis=-1, keepdims=True) + EPS) * g


def _layer_norm(x, g, b):
    mu = jnp.mean(x, axis=-1, keepdims=True)
    xc = x - mu
    var = jnp.mean(xc * xc, axis=-1, keepdims=True)
    return xc * lax.rsqrt(var + EPS) * g + b


def _dot(a, b):
    return jnp.dot(a, b, preferred_element_type=F32)


def _swiglu_half_step(x, norm_ref, wg_ref, wu_ref, wd_ref, a_scr):
    d_ff = wg_ref.shape[1]
    h = _rms(x, norm_ref[...]).astype(BF16)
    for c in range(0, d_ff, FF_CHUNK):
        w = min(FF_CHUNK, d_ff - c)
        g = _dot(h, wg_ref[:, c:c + w])
        u = _dot(h, wu_ref[:, c:c + w])
        a_scr[:, c:c + w] = (g * jax.nn.sigmoid(g) * u).astype(BF16)
    return x + 0.5 * _dot(a_scr[...], wd_ref[...])


def _token_kernel(x_ref, n1_ref, wg_ref, wu_ref, wd_ref, nmix_ref, win_ref, lng_ref, lnb_ref,
                  qn_ref, kn_ref, havg_ref,
                  x1_ref, glu_ref, u_ref, vg_ref, q_ref, k_ref, v_ref, ktail_ref, vtail_ref,
                  a_scr, *, tiles_per_seq, d_conv, d_gmlp, d_att):
    x1 = _swiglu_half_step(x_ref[...], n1_ref, wg_ref, wu_ref, wd_ref, a_scr)
    x1_ref[...] = x1
    h = _rms(x1, nmix_ref[...]).astype(BF16)

    o1 = 2 * d_conv
    o2 = o1 + 2 * d_gmlp
    o3 = o2 + d_att
    o4 = o3 + d_att

    zc = _dot(h, win_ref[:, 0:o1])
    glu_ref[...] = zc[:, :d_conv] * jax.nn.sigmoid(zc[:, d_conv:])

    uv = jax.nn.gelu(_dot(h, win_ref[:, o1:o2]))
    u_ref[...] = uv[:, :d_gmlp]
    vg_ref[...] = _layer_norm(uv[:, d_gmlp:], lng_ref[...], lnb_ref[...]).astype(vg_ref.dtype)

    def head_rms(z, g):
        ms = _dot((z * z).astype(BF16), havg_ref[...])
        return z * lax.rsqrt(ms + EPS) * g

    zq = _dot(h, win_ref[:, o2:o3])
    q_ref[...] = head_rms(zq, qn_ref[...] * (HEAD_DIM ** -0.5)).astype(BF16)
    zk = _dot(h, win_ref[:, o3:o4])
    k = head_rms(zk, kn_ref[...])
    k_ref[...] = k.astype(BF16)
    v = _dot(h, win_ref[:, o4:])
    v_ref[...] = v.astype(BF16)

    tail_rows = ktail_ref.shape[0]

    @pl.when(pl.program_id(0) % tiles_per_seq == tiles_per_seq - 1)
    def _():
        ktail_ref[...] = k[k.shape[0] - tail_rows:, :]
        vtail_ref[...] = v[v.shape[0] - tail_rows:, :]


def _resident(shape):
    return pl.BlockSpec(shape, lambda *_: (0,) * len(shape), pipeline_mode=pl.Buffered(1))


def _token_call(x, p, *, seq_len, tail_rows, vg_dtype):
    n_tok, d_model = x.shape
    tm = TOKEN_TILE
    assert n_tok % tm == 0 and (seq_len % tm == 0 or tm % seq_len == 0)
    tiles_per_seq = max(seq_len // tm, 1)
    assert tail_rows <= tm
    n_tail = (n_tok // tm // tiles_per_seq) * tail_rows
    d_ff = p['wg1'].shape[1]
    d_in = p['w_in'].shape[1]
    d_conv, d_gmlp, d_att = p['d_conv'], p['d_gmlp'], p['d_att']

    def rows(width):
        return pl.BlockSpec((tm, width), lambda i: (i, 0))

    tail_spec = pl.BlockSpec((tail_rows, d_att), lambda i: (i // tiles_per_seq, 0))
    in_specs = [
        rows(d_model), _resident((1, d_model)),
        _resident((d_model, d_ff)), _resident((d_model, d_ff)), _resident((d_ff, d_model)),
        _resident((1, d_model)), _resident((d_model, d_in)),
        _resident((1, d_gmlp)), _resident((1, d_gmlp)),
        _resident((1, d_att)), _resident((1, d_att)), _resident((d_att, d_att)),
    ]
    out_shape = [
        jax.ShapeDtypeStruct((n_tok, d_model), F32),
        jax.ShapeDtypeStruct((n_tok, d_conv), F32),
        jax.ShapeDtypeStruct((n_tok, d_gmlp), F32),
        jax.ShapeDtypeStruct((n_tok, d_gmlp), vg_dtype),
        jax.ShapeDtypeStruct((n_tok, d_att), BF16),
        jax.ShapeDtypeStruct((n_tok, d_att), BF16),
        jax.ShapeDtypeStruct((n_tok, d_att), BF16),
        jax.ShapeDtypeStruct((n_tail, d_att), F32),
        jax.ShapeDtypeStruct((n_tail, d_att), F32),
    ]
    out_specs = [rows(d_model), rows(d_conv), rows(d_gmlp), rows(d_gmlp),
                 rows(d_att), rows(d_att), rows(d_att), tail_spec, tail_spec]

    weight_bytes = (3 * _nbytes((d_model, d_ff), BF16) + _nbytes((d_model, d_in), BF16)
                    + _nbytes((d_att, d_att), BF16))
    block_bytes = 2 * (2 * _nbytes((tm, d_model), F32) + 2 * _nbytes((tm, d_conv), F32)
                       + _nbytes((tm, d_gmlp), F32) + 3 * _nbytes((tm, d_att), BF16)
                       + 2 * _nbytes((tail_rows, d_att), F32))
    scratch_bytes = _nbytes((tm, d_ff), BF16)
    temp_bytes = 6 * _nbytes((tm, d_model), F32)

    kernel = functools.partial(_token_kernel, tiles_per_seq=tiles_per_seq,
                               d_conv=d_conv, d_gmlp=d_gmlp, d_att=d_att)
    return pl.pallas_call(
        kernel,
        out_shape=out_shape,
        grid=(n_tok // tm,),
        in_specs=in_specs,
        out_specs=out_specs,
        scratch_shapes=[pltpu.VMEM((tm, d_ff), BF16)],
        compiler_params=pltpu.CompilerParams(
            dimension_semantics=("arbitrary",),
            vmem_limit_bytes=_vmem_limit(weight_bytes + block_bytes, scratch_bytes, temp_bytes)),
        name="token_kernel",
    )(x, p['n1'], p['wg1'], p['wu1'], p['wd1'], p['nmix'], p['w_in'], p['gm_ln_g'], p['gm_ln_b'],
      p['qn'], p['kn'], p['head_avg'])


def _mixer_kernel(x1_ref, gc_ref, gh_ref, u_ref, vg_ref, q_ref, kc_ref, kh_ref, vc_ref, vh_ref,
                  cw_ref, cb_ref, clg_ref, clb_ref, wcat_ref, gmb_ref, bias_ref,
                  wout_ref, n2_ref, wg_ref, wu_ref, wd_ref,
                  y_ref,
                  gbuf, kk, vv, mixed, a_scr, *, first_tile_has_no_past):
    nb_count, tt, d_model = x1_ref.shape
    d_conv = gc_ref.shape[2]
    d_gmlp = u_ref.shape[2]
    d_att = q_ref.shape[2]
    gm_rows = gmb_ref.shape[0]
    t = pl.program_id(1)
    slab = V7X_MXU_COLS
    n_slabs = d_att // slab

    head_of_lane = lax.broadcasted_iota(jnp.int32, (1, slab), 1) // HEAD_DIM
    gm_head_of_lane = lax.broadcasted_iota(jnp.int32, (1, d_gmlp), 1) // GM_HEAD_DIM
    key_index = lax.broadcasted_iota(jnp.int32, (1, BAND_LEN), 1)

    for nb in range(nb_count):
        r0 = nb * tt

        hist = gh_ref[nb]
        if first_tile_has_no_past:
            hist = jnp.where(t > 0, hist, 0.0)
        gbuf[0:CONV_HALO, :] = hist
        gbuf[CONV_HALO:, :] = gc_ref[nb]
        first_tap_row = CONV_HALO - CONV_PAD
        acc = None
        for r in range(V7X_SUBLANES):
            n_rows = tt if r == 0 else tt + V7X_SUBLANES
            part = None
            for off in range(r, CONV_HALO + 1, V7X_SUBLANES):
                w = off - first_tap_row
                if 0 <= w < CONV_WIDTH:
                    term = gbuf[off - r:off - r + n_rows, :] * cw_ref[w:w + 1, :]
                    part = term if part is None else part + term
            part = part[r:r + tt, :]
            acc = part if acc is None else acc + part
        yc = _layer_norm(acc + cb_ref[...], clg_ref[...], clb_ref[...])
        mixed[r0:r0 + tt, 0:d_conv] = (yc * jax.nn.sigmoid(yc)).astype(BF16)

        for ci in range(tt // gm_rows):
            rs = ci * gm_rows
            vgc = vg_ref[nb, rs:rs + gm_rows, :].astype(BF16)
            vstack = jnp.concatenate(
                [jnp.where(gm_head_of_lane == hh, vgc, jnp.zeros_like(vgc))
                 for hh in range(d_gmlp // GM_HEAD_DIM)], axis=0)
            s = _dot(wcat_ref[...], vstack) + gmb_ref[...]
            gm = u_ref[nb, rs:rs + gm_rows, :] * s
            mixed[r0 + rs:r0 + rs + gm_rows, d_conv:d_conv + d_gmlp] = gm.astype(BF16)

        kk[0:BAND_PAST, :] = kh_ref[nb]
        kk[BAND_PAST:, :] = kc_ref[nb]
        vv[0:BAND_PAST, :] = vh_ref[nb]
        vv[BAND_PAST:, :] = vc_ref[nb]
        for c in range(tt // CHUNK):
            cs = c * CHUNK
            for g in range(n_slabs):
                ls = g * slab
                q4 = q_ref[nb, cs:cs + CHUNK, ls:ls + slab]
                qs = jnp.concatenate(
                    [jnp.where(head_of_lane == hl, q4, jnp.zeros_like(q4))
                     for hl in range(HEADS_PER_SLAB)], axis=0)
                kb = kk[cs:cs + BAND_LEN, ls:ls + slab]
                vb = vv[cs:cs + BAND_LEN, ls:ls + slab]
                s = lax.dot_general(qs, kb, (((1,), (1,)), ((), ())), preferred_element_type=F32)
                bias = bias_ref[g]
                if first_tile_has_no_past and c < BAND_CHUNKS:
                    first_valid = jnp.where(t == 0, BAND_PAST - cs, 0)
                    bias = jnp.where(key_index >= first_valid, bias, MASK_VALUE)
                s = s + bias
                m = jnp.max(s, axis=-1, keepdims=True)
                e = jnp.exp(s - m)
                denom = jnp.sum(e, axis=-1, keepdims=True)
                o4 = _dot(e.astype(BF16), vb) * (1.0 / denom)
                o = o4[0:CHUNK, :]
                for hl in range(1, HEADS_PER_SLAB):
                    o = jnp.where(head_of_lane == hl, o4[hl * CHUNK:(hl + 1) * CHUNK, :], o)
                col = d_conv + d_gmlp + ls
                mixed[r0 + cs:r0 + cs + CHUNK, col:col + slab] = o.astype(BF16)

    x1 = x1_ref[...].reshape(nb_count * tt, d_model)
    x2 = x1 + _dot(mixed[...], wout_ref[...])
    y = _swiglu_half_step(x2, n2_ref, wg_ref, wu_ref, wd_ref, a_scr)
    y_ref[...] = y.reshape(nb_count, tt, d_model)


def _mixer_call(x1, glu, glu_hist, u, vg, q, k, k_hist, v, v_hist, p, *, nb, tt, self_history):
    n_seq, seq_len, d_model = x1.shape
    d_conv, d_gmlp, d_att = p['d_conv'], p['d_gmlp'], p['d_att']
    d_ff = p['wg2'].shape[1]
    assert n_seq % nb == 0 and seq_len % tt == 0 and tt % CHUNK == 0
    gm_rows = p['wcat'].shape[0]
    assert tt % gm_rows == 0
    n_t = seq_len // tt
    rows = nb * tt

    def cur(width):
        return pl.BlockSpec((nb, tt, width), lambda b, t: (b, t, 0))

    if self_history:
        assert tt % BAND_PAST == 0 and glu_hist is None and k_hist is None and v_hist is None
        glu_hist, k_hist, v_hist = glu, k, v
        gh_spec = pl.BlockSpec((nb, CONV_HALO, d_conv),
                               lambda b, t: (b, jnp.maximum(t * (tt // CONV_HALO) - 1, 0), 0))
        past_spec = pl.BlockSpec((nb, BAND_PAST, d_att),
                                 lambda b, t: (b, jnp.maximum(t * (tt // BAND_PAST) - 1, 0), 0))
    else:
        assert n_t == 1
        gh_spec = pl.BlockSpec((nb, CONV_HALO, d_conv), lambda b, t: (b, 0, 0))
        past_spec = pl.BlockSpec((nb, BAND_PAST, d_att), lambda b, t: (b, 0, 0))

    n_slabs = d_att // V7X_MXU_COLS
    in_specs = [
        cur(d_model), cur(d_conv), gh_spec, cur(d_gmlp), cur(d_gmlp), cur(d_att),
        cur(d_att), past_spec, cur(d_att), past_spec,
        _resident((CONV_WIDTH, d_conv)), _resident((1, d_conv)), _resident((1, d_conv)), _resident((1, d_conv)),
        _resident((gm_rows, (d_gmlp // GM_HEAD_DIM) * gm_rows)), _resident((gm_rows, d_gmlp)),
        _resident((n_slabs, HEADS_PER_SLAB * CHUNK, BAND_LEN)),
        _resident((d_model, d_model)), _resident((1, d_model)),
        _resident((d_model, d_ff)), _resident((d_model, d_ff)), _resident((d_ff, d_model)),
    ]
    scratch = [
        pltpu.VMEM((CONV_HALO + tt, d_conv), F32),
        pltpu.VMEM((BAND_PAST + tt, d_att), BF16),
        pltpu.VMEM((BAND_PAST + tt, d_att), BF16),
        pltpu.VMEM((rows, d_model), BF16),
        pltpu.VMEM((rows, d_ff), BF16),
    ]
    weight_bytes = (3 * _nbytes((d_model, d_ff), BF16) + _nbytes((d_model, d_model), BF16)
                    + _nbytes((n_slabs, HEADS_PER_SLAB * CHUNK, BAND_LEN), F32)
                    + _nbytes((gm_rows, 4 * gm_rows), BF16) + _nbytes((gm_rows, d_gmlp), F32))
    block_bytes = 2 * (2 * _nbytes((rows, d_model), F32) + 2 * _nbytes((rows, d_conv), F32)
                       + _nbytes((rows, d_gmlp), F32) + 3 * _nbytes((rows, d_att), BF16)
                       + 2 * _nbytes((nb, BAND_PAST, d_att), BF16) + _nbytes((nb, CONV_HALO, d_conv), F32))
    scratch_bytes = (_nbytes((CONV_HALO + tt, d_conv), F32) + 2 * _nbytes((BAND_PAST + tt, d_att), BF16)
                     + _nbytes((rows, d_model), BF16) + _nbytes((rows, d_ff), BF16))
    temp_bytes = 6 * _nbytes((rows, d_model), F32)

    kernel = functools.partial(_mixer_kernel, first_tile_has_no_past=self_history)
    return pl.pallas_call(
        kernel,
        out_shape=jax.ShapeDtypeStruct((n_seq, seq_len, d_model), F32),
        grid=(n_seq // nb, n_t),
        in_specs=in_specs,
        out_specs=cur(d_model),
        scratch_shapes=scratch,
        compiler_params=pltpu.CompilerParams(
            dimension_semantics=("arbitrary", "arbitrary"),
            vmem_limit_bytes=_vmem_limit(weight_bytes + block_bytes, scratch_bytes, temp_bytes)),
        name="mixer_kernel",
    )(x1, glu, glu_hist, u, vg, q, k, k_hist, v, v_hist,
      p['conv_w'], p['conv_b'], p['conv_ln_g'], p['conv_ln_b'], p['wcat'], p['gmb'], p['bias_tab'],
      p['w_out'], p['n2'], p['wg2'], p['wu2'], p['wd2'])


def _row(v):
    return v.reshape(1, -1).astype(F32)


def _gmlp_tables(ws, bs, rows):
    n_heads = ws.shape[0]
    w = jnp.tril(ws[:, :rows, :rows])
    wcat = jnp.transpose(w, (1, 0, 2)).reshape(rows, n_heads * rows).astype(BF16)
    gmb = jnp.repeat(bs[:, :rows].T, GM_HEAD_DIM, axis=1).astype(F32)
    return wcat, gmb


def _band_bias(rel_bias):
    n_heads = rel_bias.shape[0]
    lo, hi = -(CHUNK - 1), BAND_LEN - 1
    assert -REL_CLIP <= lo and hi >= REL_CLIP
    by_rel = jnp.concatenate(
        [rel_bias[:, lo + REL_CLIP:], jnp.broadcast_to(rel_bias[:, -1:], (n_heads, hi - REL_CLIP))],
        axis=1).astype(F32)
    rev = by_rel[:, ::-1]
    tab = jnp.stack([rev[:, CHUNK - 1 - i:CHUNK - 1 - i + BAND_LEN] for i in range(CHUNK)], axis=1)
    return tab.reshape(n_heads // HEADS_PER_SLAB, HEADS_PER_SLAB * CHUNK, BAND_LEN)


def kernel(x_prompt, x_sample, cache_conv, cache_k, cache_v, ffn1_norm, ffn1_w_gate, ffn1_w_up, ffn1_w_down, mix_norm, w_in, conv_w, conv_b, conv_ln_g, conv_ln_b, gmlp_ln_g, gmlp_ln_b, gmlp_ws, gmlp_b, q_norm, k_norm, rel_bias, w_out, ffn2_norm, ffn2_w_gate, ffn2_w_up, ffn2_w_down):
    batch, seq, d_model = x_prompt.shape
    dec_batch, dec_seq, _ = x_sample.shape
    depth = w_in.shape[0]
    d_conv = conv_w.shape[2]
    d_gmlp = gmlp_ln_g.shape[1]
    n_att_heads = rel_bias.shape[1]
    d_att = n_att_heads * HEAD_DIM
    assert cache_k.shape[2] == BAND_PAST and cache_conv.shape[2] == CONV_PAD
    assert dec_seq == CHUNK and dec_seq <= GM_CHUNK

    head_avg = jnp.kron(jnp.eye(n_att_heads, dtype=F32),
                        jnp.full((HEAD_DIM, HEAD_DIM), 1.0 / HEAD_DIM, F32)).astype(BF16)

    xp = x_prompt.reshape(batch * seq, d_model)
    xs = x_sample.reshape(dec_batch * dec_seq, d_model)
    outs = {name: [] for name in ('p_conv', 'p_k', 'p_v', 's_conv', 's_k', 's_v', 's_gv')}
    sample_streams_per_step = TOKEN_TILE // dec_seq // 2

    for l in range(depth):
        p = dict(
            d_conv=d_conv, d_gmlp=d_gmlp, d_att=d_att,
            n1=_row(ffn1_norm[l]), wg1=ffn1_w_gate[l].astype(BF16), wu1=ffn1_w_up[l].astype(BF16),
            wd1=ffn1_w_down[l].astype(BF16),
            nmix=_row(mix_norm[l]), w_in=w_in[l].astype(BF16),
            gm_ln_g=_row(gmlp_ln_g[l]), gm_ln_b=_row(gmlp_ln_b[l]),
            qn=_row(jnp.tile(q_norm[l], n_att_heads)), kn=_row(jnp.tile(k_norm[l], n_att_heads)),
            head_avg=head_avg,
            conv_w=conv_w[l].astype(F32), conv_b=_row(conv_b[l]),
            conv_ln_g=_row(conv_ln_g[l]), conv_ln_b=_row(conv_ln_b[l]),
            bias_tab=_band_bias(rel_bias[l]),
            w_out=w_out[l].astype(BF16), n2=_row(ffn2_norm[l]), wg2=ffn2_w_gate[l].astype(BF16),
            wu2=ffn2_w_up[l].astype(BF16), wd2=ffn2_w_down[l].astype(BF16),
        )

        x1, glu, u, vg, q, k, v, k_tail, v_tail = _token_call(
            xp, p, seq_len=seq, tail_rows=min(BAND_PAST, seq), vg_dtype=BF16)
        p['wcat'], p['gmb'] = _gmlp_tables(gmlp_ws[l], gmlp_b[l], GM_CHUNK)
        seq3 = lambda a: a.reshape(batch, seq, a.shape[-1])
        xp = _mixer_call(seq3(x1), seq3(glu), None, seq3(u), seq3(vg), seq3(q), seq3(k), None, seq3(v), None,
                         p, nb=1, tt=TOKEN_TILE, self_history=True).reshape(batch * seq, d_model)
        outs['p_conv'].append(seq3(glu)[:, seq - CONV_PAD:])
        outs['p_k'].append(k_tail.reshape(batch, -1, n_att_heads, HEAD_DIM))
        outs['p_v'].append(v_tail.reshape(batch, -1, n_att_heads, HEAD_DIM))

        x1, glu, u, vg, q, k, v, k_new, v_new = _token_call(
            xs, p, seq_len=dec_seq, tail_rows=TOKEN_TILE, vg_dtype=F32)
        p['wcat'], p['gmb'] = _gmlp_tables(gmlp_ws[l], gmlp_b[l], dec_seq)
        dec3 = lambda a: a.reshape(dec_batch, dec_seq, a.shape[-1])
        conv_hist = jnp.pad(cache_conv[l].astype(F32), ((0, 0), (CONV_HALO - CONV_PAD, 0), (0, 0)))
        k_hist = cache_k[l].reshape(dec_batch, BAND_PAST, d_att).astype(BF16)
        v_hist = cache_v[l].reshape(dec_batch, BAND_PAST, d_att).astype(BF16)
        xs = _mixer_call(dec3(x1), dec3(glu), conv_hist, dec3(u), dec3(vg), dec3(q), dec3(k), k_hist,
                         dec3(v), v_hist, p, nb=sample_streams_per_step, tt=dec_seq,
                         self_history=False).reshape(dec_batch * dec_seq, d_model)
        outs['s_conv'].append(dec3(glu)[:, dec_seq - CONV_PAD:])
        outs['s_k'].append(k_new.reshape(dec_batch, dec_seq, n_att_heads, HEAD_DIM))
        outs['s_v'].append(v_new.reshape(dec_batch, dec_seq, n_att_heads, HEAD_DIM))
        outs['s_gv'].append(vg.reshape(dec_batch, dec_seq, d_gmlp // GM_HEAD_DIM, GM_HEAD_DIM))

    return (xp.reshape(batch, seq, d_model), xs.reshape(dec_batch, dec_seq, d_model),
            jnp.stack(outs['p_conv']), jnp.stack(outs['p_k']), jnp.stack(outs['p_v']),
            jnp.stack(outs['s_conv']), jnp.stack(outs['s_k']), jnp.stack(outs['s_v']),
            jnp.stack(outs['s_gv']))
```

```python
import functools

import jax
import jax.numpy as jnp
from jax import lax
from jax.experimental import pallas as pl
from jax.experimental.pallas import tpu as pltpu

F32 = jnp.float32
BF16 = jnp.bfloat16

EPS = 1e-6
CHUNK = 64
BAND_CHUNKS = 8
BAND_PAST = BAND_CHUNKS * CHUNK
BAND_LEN = BAND_PAST + CHUNK
REL_CLIP = 128
CONV_WIDTH = 31
CONV_PAD = CONV_WIDTH - 1
CONV_HALO = 32
HEAD_DIM = 64
GM_HEAD_DIM = 64
GM_CHUNK = 128
MASK_VALUE = -1e30

V7X_LANES = 128
V7X_SUBLANES = 8
V7X_MXU_COLS = 256
V7X_VMEM_BYTES = 64 * 1024 * 1024
HEADS_PER_SLAB = V7X_MXU_COLS // HEAD_DIM

TOKEN_TILE = 512
FF_CHUNK = 512


def _vmem_limit(block_bytes, scratch_bytes, temp_bytes):
    return int(min(V7X_VMEM_BYTES - (4 << 20), block_bytes + scratch_bytes + temp_bytes))


def _nbytes(shape, dtype):
    n = 1
    for s in shape:
        n *= s
    return n * jnp.dtype(dtype).itemsize


def _rms(x, g):
    return x * lax.rsqrt(jnp.mean(x * x, axis=-1, keepdims=True) + EPS) * g


def _layer_norm(x, g, b):
    mu = jnp.mean(x, axis=-1, keepdims=True)
    xc = x - mu
    var = jnp.mean(xc * xc, axis=-1, keepdims=True)
    return xc * lax.rsqrt(var + EPS) * g + b


def _dot(a, b):
    return jnp.dot(a, b, preferred_element_type=F32)


def _swiglu_half_step(x, norm_ref, wg_ref, wu_ref, wd_ref, a_scr):
    d_ff = wg_ref.shape[1]
    h = _rms(x, norm_ref[...]).astype(BF16)
    for c in range(0, d_ff, FF_CHUNK):
        w = min(FF_CHUNK, d_ff - c)
        g = _dot(h, wg_ref[:, c:c + w])
        u = _dot(h, wu_ref[:, c:c + w])
        a_scr[:, c:c + w] = (g * jax.nn.sigmoid(g) * u).astype(BF16)
    return x + 0.5 * _dot(a_scr[...], wd_ref[...])


def _token_kernel(x_ref, n1_ref, wg_ref, wu_ref, wd_ref, nmix_ref, win_ref, lng_ref, lnb_ref,
                  qn_ref, kn_ref, havg_ref,
                  x1_ref, glu_ref, u_ref, vg_ref, q_ref, k_ref, v_ref, ktail_ref, vtail_ref,
                  a_scr, *, tiles_per_seq, d_conv, d_gmlp, d_att):
    x1 = _swiglu_half_step(x_ref[...], n1_ref, wg_ref, wu_ref, wd_ref, a_scr)
    x1_ref[...] = x1
    h = _rms(x1, nmix_ref[...]).astype(BF16)

    o1 = 2 * d_conv
    o2 = o1 + 2 * d_gmlp
    o3 = o2 + d_att
    o4 = o3 + d_att

    zc = _dot(h, win_ref[:, 0:o1])
    glu_ref[...] = zc[:, :d_conv] * jax.nn.sigmoid(zc[:, d_conv:])

    uv = jax.nn.gelu(_dot(h, win_ref[:, o1:o2]))
    u_ref[...] = uv[:, :d_gmlp]
    vg_ref[...] = _layer_norm(uv[:, d_gmlp:], lng_ref[...], lnb_ref[...]).astype(vg_ref.dtype)

    def head_rms(z, g):
        ms = _dot((z * z).astype(BF16), havg_ref[...])
        return z * lax.rsqrt(ms + EPS) * g

    zq = _dot(h, win_ref[:, o2:o3])
    q_ref[...] = head_rms(zq, qn_ref[...] * (HEAD_DIM ** -0.5)).astype(BF16)
    zk = _dot(h, win_ref[:, o3:o4])
    k = head_rms(zk, kn_ref[...])
    k_ref[...] = k.astype(BF16)
    v = _dot(h, win_ref[:, o4:])
    v_ref[...] = v.astype(BF16)

    tail_rows = ktail_ref.shape[0]

    @pl.when(pl.program_id(0) % tiles_per_seq == tiles_per_seq - 1)
    def _():
        ktail_ref[...] = k[k.shape[0] - tail_rows:, :]
        vtail_ref[...] = v[v.shape[0] - tail_rows:, :]


def _resident(shape):
    return pl.BlockSpec(shape, lambda *_: (0,) * len(shape), pipeline_mode=pl.Buffered(1))


def _token_call(x, p, *, seq_len, tail_rows, vg_dtype):
    n_tok, d_model = x.shape
    tm = TOKEN_TILE
    assert n_tok % tm == 0 and (seq_len % tm == 0 or tm % seq_len == 0)
    tiles_per_seq = max(seq_len // tm, 1)
    assert tail_rows <= tm
    n_tail = (n_tok // tm // tiles_per_seq) * tail_rows
    d_ff = p['wg1'].shape[1]
    d_in = p['w_in'].shape[1]
    d_conv, d_gmlp, d_att = p['d_conv'], p['d_gmlp'], p['d_att']

    def rows(width):
        return pl.BlockSpec((tm, width), lambda i: (i, 0))

    tail_spec = pl.BlockSpec((tail_rows, d_att), lambda i: (i // tiles_per_seq, 0))
    in_specs = [
        rows(d_model), _resident((1, d_model)),
        _resident((d_model, d_ff)), _resident((d_model, d_ff)), _resident((d_ff, d_model)),
        _resident((1, d_model)), _resident((d_model, d_in)),
        _resident((1, d_gmlp)), _resident((1, d_gmlp)),
        _resident((1, d_att)), _resident((1, d_att)), _resident((d_att, d_att)),
    ]
    out_shape = [
        jax.ShapeDtypeStruct((n_tok, d_model), F32),
        jax.ShapeDtypeStruct((n_tok, d_conv), F32),
        jax.ShapeDtypeStruct((n_tok, d_gmlp), F32),
        jax.ShapeDtypeStruct((n_tok, d_gmlp), vg_dtype),
        jax.ShapeDtypeStruct((n_tok, d_att), BF16),
        jax.ShapeDtypeStruct((n_tok, d_att), BF16),
        jax.ShapeDtypeStruct((n_tok, d_att), BF16),
        jax.ShapeDtypeStruct((n_tail, d_att), F32),
        jax.ShapeDtypeStruct((n_tail, d_att), F32),
    ]
    out_specs = [rows(d_model), rows(d_conv), rows(d_gmlp), rows(d_gmlp),
                 rows(d_att), rows(d_att), rows(d_att), tail_spec, tail_spec]

    weight_bytes = (3 * _nbytes((d_model, d_ff), BF16) + _nbytes((d_model, d_in), BF16)
                    + _nbytes((d_att, d_att), BF16))
    block_bytes = 2 * (2 * _nbytes((tm, d_model), F32) + 2 * _nbytes((tm, d_conv), F32)
                       + _nbytes((tm, d_gmlp), F32) + 3 * _nbytes((tm, d_att), BF16)
                       + 2 * _nbytes((tail_rows, d_att), F32))
    scratch_bytes = _nbytes((tm, d_ff), BF16)
    temp_bytes = 6 * _nbytes((tm, d_model), F32)

    kernel = functools.partial(_token_kernel, tiles_per_seq=tiles_per_seq,
                               d_conv=d_conv, d_gmlp=d_gmlp, d_att=d_att)
    return pl.pallas_call(
        kernel,
        out_shape=out_shape,
        grid=(n_tok // tm,),
        in_specs=in_specs,
        out_specs=out_specs,
        scratch_shapes=[pltpu.VMEM((tm, d_ff), BF16)],
        compiler_params=pltpu.CompilerParams(
            dimension_semantics=("arbitrary",),
            vmem_limit_bytes=_vmem_limit(weight_bytes + block_bytes, scratch_bytes, temp_bytes)),
        name="token_kernel",
    )(x, p['n1'], p['wg1'], p['wu1'], p['wd1'], p['nmix'], p['w_in'], p['gm_ln_g'], p['gm_ln_b'],
      p['qn'], p['kn'], p['head_avg'])


def _mixer_kernel(x1_ref, gc_ref, gh_ref, u_ref, vg_ref, q_ref, kc_ref, kh_ref, vc_ref, vh_ref,
                  cw_ref, cb_ref, clg_ref, clb_ref, wcat_ref, gmb_ref, bias_ref,
                  wout_ref, n2_ref, wg_ref, wu_ref, wd_ref,
                  y_ref,
                  gbuf, kk, vv, mixed, a_scr, *, first_tile_has_no_past):
    nb_count, tt, d_model = x1_ref.shape
    d_conv = gc_ref.shape[2]
    d_gmlp = u_ref.shape[2]
    d_att = q_ref.shape[2]
    gm_rows = gmb_ref.shape[0]
    t = pl.program_id(1)
    slab = V7X_MXU_COLS
    n_slabs = d_att // slab

    head_of_lane = lax.broadcasted_iota(jnp.int32, (1, slab), 1) // HEAD_DIM
    gm_head_of_lane = lax.broadcasted_iota(jnp.int32, (1, d_gmlp), 1) // GM_HEAD_DIM
    key_index = lax.broadcasted_iota(jnp.int32, (1, BAND_LEN), 1)

    for nb in range(nb_count):
        r0 = nb * tt

        hist = gh_ref[nb]
        if first_tile_has_no_past:
            hist = jnp.where(t > 0, hist, 0.0)
        gbuf[0:CONV_HALO, :] = hist
        gbuf[CONV_HALO:, :] = gc_ref[nb]
        first_tap_row = CONV_HALO - CONV_PAD
        acc = None
        for r in range(V7X_SUBLANES):
            n_rows = tt if r == 0 else tt + V7X_SUBLANES
            part = None
            for off in range(r, CONV_HALO + 1, V7X_SUBLANES):
                w = off - first_tap_row
                if 0 <= w < CONV_WIDTH:
                    term = gbuf[off - r:off - r + n_rows, :] * cw_ref[w:w + 1, :]
                    part = term if part is None else part + term
            part = part[r:r + tt, :]
            acc = part if acc is None else acc + part
        yc = _layer_norm(acc + cb_ref[...], clg_ref[...], clb_ref[...])
        mixed[r0:r0 + tt, 0:d_conv] = (yc * jax.nn.sigmoid(yc)).astype(BF16)

        for ci in range(tt // gm_rows):
            rs = ci * gm_rows
            vgc = vg_ref[nb, rs:rs + gm_rows, :].astype(BF16)
            vstack = jnp.concatenate(
                [jnp.where(gm_head_of_lane == hh, vgc, jnp.zeros_like(vgc))
                 for hh in range(d_gmlp // GM_HEAD_DIM)], axis=0)
            s = _dot(wcat_ref[...], vstack) + gmb_ref[...]
            gm = u_ref[nb, rs:rs + gm_rows, :] * s
            mixed[r0 + rs:r0 + rs + gm_rows, d_conv:d_conv + d_gmlp] = gm.astype(BF16)

        kk[0:BAND_PAST, :] = kh_ref[nb]
        kk[BAND_PAST:, :] = kc_ref[nb]
        vv[0:BAND_PAST, :] = vh_ref[nb]
        vv[BAND_PAST:, :] = vc_ref[nb]
        for c in range(tt // CHUNK):
            cs = c * CHUNK
            for g in range(n_slabs):
                ls = g * slab
                q4 = q_ref[nb, cs:cs + CHUNK, ls:ls + slab]
                qs = jnp.concatenate(
                    [jnp.where(head_of_lane == hl, q4, jnp.zeros_like(q4))
                     for hl in range(HEADS_PER_SLAB)], axis=0)
                kb = kk[cs:cs + BAND_LEN, ls:ls + slab]
                vb = vv[cs:cs + BAND_LEN, ls:ls + slab]
                s = lax.dot_general(qs, kb, (((1,), (1,)), ((), ())), preferred_element_type=F32)
                bias = bias_ref[g]
                if first_tile_has_no_past and c < BAND_CHUNKS:
                    first_valid = jnp.where(t == 0, BAND_PAST - cs, 0)
                    bias = jnp.where(key_index >= first_valid, bias, MASK_VALUE)
                s = s + bias
                m = jnp.max(s, axis=-1, keepdims=True)
                e = jnp.exp(s - m)
                denom = jnp.sum(e, axis=-1, keepdims=True)
                o4 = _dot(e.astype(BF16), vb) * (1.0 / denom)
                o = o4[0:CHUNK, :]
                for hl in range(1, HEADS_PER_SLAB):
                    o = jnp.where(head_of_lane == hl, o4[hl * CHUNK:(hl + 1) * CHUNK, :], o)
                col = d_conv + d_gmlp + ls
                mixed[r0 + cs:r0 + cs + CHUNK, col:col + slab] = o.astype(BF16)

    x1 = x1_ref[...].reshape(nb_count * tt, d_model)
    x2 = x1 + _dot(mixed[...], wout_ref[...])
    y = _swiglu_half_step(x2, n2_ref, wg_ref, wu_ref, wd_ref, a_scr)
    y_ref[...] = y.reshape(nb_count, tt, d_model)


def _mixer_call(x1, glu, glu_hist, u, vg, q, k, k_hist, v, v_hist, p, *, nb, tt, self_history):
    n_seq, seq_len, d_model = x1.shape
    d_conv, d_gmlp, d_att = p['d_conv'], p['d_gmlp'], p['d_att']
    d_ff = p['wg2'].shape[1]
    assert n_seq % nb == 0 and seq_len % tt == 0 and tt % CHUNK == 0
    gm_rows = p['wcat'].shape[0]
    assert tt % gm_rows == 0
    n_t = seq_len // tt
    rows = nb * tt

    def cur(width):
        return pl.BlockSpec((nb, tt, width), lambda b, t: (b, t, 0))

    if self_history:
        assert tt % BAND_PAST == 0 and glu_hist is None and k_hist is None and v_hist is None
        glu_hist, k_hist, v_hist = glu, k, v
        gh_spec = pl.BlockSpec((nb, CONV_HALO, d_conv),
                               lambda b, t: (b, jnp.maximum(t * (tt // CONV_HALO) - 1, 0), 0))
        past_spec = pl.BlockSpec((nb, BAND_PAST, d_att),
                                 lambda b, t: (b, jnp.maximum(t * (tt // BAND_PAST) - 1, 0), 0))
    else:
        assert n_t == 1
        gh_spec = pl.BlockSpec((nb, CONV_HALO, d_conv), lambda b, t: (b, 0, 0))
        past_spec = pl.BlockSpec((nb, BAND_PAST, d_att), lambda b, t: (b, 0, 0))

    n_slabs = d_att // V7X_MXU_COLS
    in_specs = [
        cur(d_model), cur(d_conv), gh_spec, cur(d_gmlp), cur(d_gmlp), cur(d_att),
        cur(d_att), past_spec, cur(d_att), past_spec,
        _resident((CONV_WIDTH, d_conv)), _resident((1, d_conv)), _resident((1, d_conv)), _resident((1, d_conv)),
        _resident((gm_rows, (d_gmlp // GM_HEAD_DIM) * gm_rows)), _resident((gm_rows, d_gmlp)),
        _resident((n_slabs, HEADS_PER_SLAB * CHUNK, BAND_LEN)),
        _resident((d_model, d_model)), _resident((1, d_model)),
        _resident((d_model, d_ff)), _resident((d_model, d_ff)), _resident((d_ff, d_model)),
    ]
    scratch = [
        pltpu.VMEM((CONV_HALO + tt, d_conv), F32),
        pltpu.VMEM((BAND_PAST + tt, d_att), BF16),
        pltpu.VMEM((BAND_PAST + tt, d_att), BF16),
        pltpu.VMEM((rows, d_model), BF16),
        pltpu.VMEM((rows, d_ff), BF16),
    ]
    weight_bytes = (3 * _nbytes((d_model, d_ff), BF16) + _nbytes((d_model, d_model), BF16)
                    + _nbytes((n_slabs, HEADS_PER_SLAB * CHUNK, BAND_LEN), F32)
                    + _nbytes((gm_rows, 4 * gm_rows), BF16) + _nbytes((gm_rows, d_gmlp), F32))
    block_bytes = 2 * (2 * _nbytes((rows, d_model), F32) + 2 * _nbytes((rows, d_conv), F32)
                       + _nbytes((rows, d_gmlp), F32) + 3 * _nbytes((rows, d_att), BF16)
                       + 2 * _nbytes((nb, BAND_PAST, d_att), BF16) + _nbytes((nb, CONV_HALO, d_conv), F32))
    scratch_bytes = (_nbytes((CONV_HALO + tt, d_conv), F32) + 2 * _nbytes((BAND_PAST + tt, d_att), BF16)
                     + _nbytes((rows, d_model), BF16) + _nbytes((rows, d_ff), BF16))
    temp_bytes = 6 * _nbytes((rows, d_model), F32)

    kernel = functools.partial(_mixer_kernel, first_tile_has_no_past=self_history)
    return pl.pallas_call(
        kernel,
        out_shape=jax.ShapeDtypeStruct((n_seq, seq_len, d_model), F32),
        grid=(n_seq // nb, n_t),
        in_specs=in_specs,
        out_specs=cur(d_model),
        scratch_shapes=scratch,
        compiler_params=pltpu.CompilerParams(
            dimension_semantics=("arbitrary", "arbitrary"),
            vmem_limit_bytes=_vmem_limit(weight_bytes + block_bytes, scratch_bytes, temp_bytes)),
        name="mixer_kernel",
    )(x1, glu, glu_hist, u, vg, q, k, k_hist, v, v_hist,
      p['conv_w'], p['conv_b'], p['conv_ln_g'], p['conv_ln_b'], p['wcat'], p['gmb'], p['bias_tab'],
      p['w_out'], p['n2'], p['wg2'], p['wu2'], p['wd2'])


def _row(v):
    return v.reshape(1, -1).astype(F32)


def _gmlp_tables(ws, bs, rows):
    n_heads = ws.shape[0]
    w = jnp.tril(ws[:, :rows, :rows])
    wcat = jnp.transpose(w, (1, 0, 2)).reshape(rows, n_heads * rows).astype(BF16)
    gmb = jnp.repeat(bs[:, :rows].T, GM_HEAD_DIM, axis=1).astype(F32)
    return wcat, gmb


def _band_bias(rel_bias):
    n_heads = rel_bias.shape[0]
    lo, hi = -(CHUNK - 1), BAND_LEN - 1
    assert -REL_CLIP <= lo and hi >= REL_CLIP
    by_rel = jnp.concatenate(
        [rel_bias[:, lo + REL_CLIP:], jnp.broadcast_to(rel_bias[:, -1:], (n_heads, hi - REL_CLIP))],
        axis=1).astype(F32)
    width = hi - lo + 1
    padded = jnp.pad(by_rel[:, ::-1], ((0, 0), (0, 1)))
    skew = jnp.tile(padded, (1, CHUNK))[:, :CHUNK * width].reshape(n_heads, CHUNK, width)
    tab = skew[:, :, CHUNK - 1:]
    return tab.reshape(n_heads // HEADS_PER_SLAB, HEADS_PER_SLAB * CHUNK, BAND_LEN)


def kernel(x_prompt, x_sample, cache_conv, cache_k, cache_v, ffn1_norm, ffn1_w_gate, ffn1_w_up, ffn1_w_down, mix_norm, w_in, conv_w, conv_b, conv_ln_g, conv_ln_b, gmlp_ln_g, gmlp_ln_b, gmlp_ws, gmlp_b, q_norm, k_norm, rel_bias, w_out, ffn2_norm, ffn2_w_gate, ffn2_w_up, ffn2_w_down):
    batch, seq, d_model = x_prompt.shape
    dec_batch, dec_seq, _ = x_sample.shape
    depth = w_in.shape[0]
    d_conv = conv_w.shape[2]
    d_gmlp = gmlp_ln_g.shape[1]
    n_att_heads = rel_bias.shape[1]
    d_att = n_att_heads * HEAD_DIM
    assert cache_k.shape[2] == BAND_PAST and cache_conv.shape[2] == CONV_PAD
    assert dec_seq == CHUNK and dec_seq <= GM_CHUNK

    head_avg = jnp.kron(jnp.eye(n_att_heads, dtype=F32),
                        jnp.full((HEAD_DIM, HEAD_DIM), 1.0 / HEAD_DIM, F32)).astype(BF16)

    xp = x_prompt.reshape(batch * seq, d_model)
    xs = x_sample.reshape(dec_batch * dec_seq, d_model)
    outs = {name: [] for name in ('p_conv', 'p_k', 'p_v', 's_conv', 's_k', 's_v', 's_gv')}
    sample_streams_per_step = TOKEN_TILE // dec_seq // 2

    for l in range(depth):
        p = dict(
            d_conv=d_conv, d_gmlp=d_gmlp, d_att=d_att,
            n1=_row(ffn1_norm[l]), wg1=ffn1_w_gate[l].astype(BF16), wu1=ffn1_w_up[l].astype(BF16),
            wd1=ffn1_w_down[l].astype(BF16),
            nmix=_row(mix_norm[l]), w_in=w_in[l].astype(BF16),
            gm_ln_g=_row(gmlp_ln_g[l]), gm_ln_b=_row(gmlp_ln_b[l]),
            qn=_row(jnp.tile(q_norm[l], n_att_heads)), kn=_row(jnp.tile(k_norm[l], n_att_heads)),
            head_avg=head_avg,
            conv_w=conv_w[l].astype(F32), conv_b=_row(conv_b[l]),
            conv_ln_g=_row(conv_ln_g[l]), conv_ln_b=_row(conv_ln_b[l]),
            bias_tab=_band_bias(rel_bias[l]),
            w_out=w_out[l].astype(BF16), n2=_row(ffn2_norm[l]), wg2=ffn2_w_gate[l].astype(BF16),
            wu2=ffn2_w_up[l].astype(BF16), wd2=ffn2_w_down[l].astype(BF16),
        )

        x1, glu, u, vg, q, k, v, k_tail, v_tail = _token_call(
            xp, p, seq_len=seq, tail_rows=min(BAND_PAST, seq), vg_dtype=BF16)
        p['wcat'], p['gmb'] = _gmlp_tables(gmlp_ws[l], gmlp_b[l], GM_CHUNK)
        seq3 = lambda a: a.reshape(batch, seq, a.shape[-1])
        xp = _mixer_call(seq3(x1), seq3(glu), None, seq3(u), seq3(vg), seq3(q), seq3(k), None, seq3(v), None,
                         p, nb=1, tt=TOKEN_TILE, self_history=True).reshape(batch * seq, d_model)
        outs['p_conv'].append(seq3(glu)[:, seq - CONV_PAD:])
        outs['p_k'].append(k_tail.reshape(batch, -1, n_att_heads, HEAD_DIM))
        outs['p_v'].append(v_tail.reshape(batch, -1, n_att_heads, HEAD_DIM))

        x1, glu, u, vg, q, k, v, k_new, v_new = _token_call(
            xs, p, seq_len=dec_seq, tail_rows=TOKEN_TILE, vg_dtype=F32)
        p['wcat'], p['gmb'] = _gmlp_tables(gmlp_ws[l], gmlp_b[l], dec_seq)
        dec3 = lambda a: a.reshape(dec_batch, dec_seq, a.shape[-1])
        conv_hist = jnp.pad(cache_conv[l].astype(F32), ((0, 0), (CONV_HALO - CONV_PAD, 0), (0, 0)))
        k_hist = cache_k[l].reshape(dec_batch, BAND_PAST, d_att).astype(BF16)
        v_hist = cache_v[l].reshape(dec_batch, BAND_PAST, d_att).astype(BF16)
        xs = _mixer_call(dec3(x1), dec3(glu), conv_hist, dec3(u), dec3(vg), dec3(q), dec3(k), k_hist,
                         dec3(v), v_hist, p, nb=sample_streams_per_step, tt=dec_seq,
                         self_history=False).reshape(dec_batch * dec_seq, d_model)
        outs['s_conv'].append(dec3(glu)[:, dec_seq - CONV_PAD:])
        outs['s_k'].append(k_new.reshape(dec_batch, dec_seq, n_att_heads, HEAD_DIM))
        outs['s_v'].append(v_new.reshape(dec_batch, dec_seq, n_att_heads, HEAD_DIM))
        outs['s_gv'].append(vg.reshape(dec_batch, dec_seq, d_gmlp // GM_HEAD_DIM, GM_HEAD_DIM))

    return (xp.reshape(batch, seq, d_model), xs.reshape(dec_batch, dec_seq, d_model),
            jnp.stack(outs['p_conv']), jnp.stack(outs['p_k']), jnp.stack(outs['p_v']),
            jnp.stack(outs['s_conv']), jnp.stack(outs['s_k']), jnp.stack(outs['s_v']),
            jnp.stack(outs['s_gv']))
```

```python
import functools

import jax
import jax.numpy as jnp
from jax import lax
from jax.experimental import pallas as pl
from jax.experimental.pallas import tpu as pltpu

F32 = jnp.float32
BF16 = jnp.bfloat16

EPS = 1e-6
CHUNK = 64
BAND_CHUNKS = 8
BAND_PAST = BAND_CHUNKS * CHUNK
BAND_LEN = BAND_PAST + CHUNK
REL_CLIP = 128
CONV_WIDTH = 31
CONV_PAD = CONV_WIDTH - 1
CONV_HALO = 32
HEAD_DIM = 64
GM_HEAD_DIM = 64
GM_CHUNK = 128
MASK_VALUE = -1e30

V7X_LANES = 128
V7X_SUBLANES = 8
V7X_MXU_COLS = 256
V7X_VMEM_BYTES = 64 * 1024 * 1024
HEADS_PER_SLAB = V7X_MXU_COLS // HEAD_DIM

TOKEN_TILE = 512
FF_CHUNK = 512
ATT_CHUNKS_PER_DOT = 1


def _vmem_limit(block_bytes, scratch_bytes, temp_bytes):
    return int(min(V7X_VMEM_BYTES - (4 << 20), block_bytes + scratch_bytes + temp_bytes))


def _nbytes(shape, dtype):
    n = 1
    for s in shape:
        n *= s
    return n * jnp.dtype(dtype).itemsize


def _rms(x, g):
    return x * lax.rsqrt(jnp.mean(x * x, axis=-1, keepdims=True) + EPS) * g


def _layer_norm(x, g, b):
    mu = jnp.mean(x, axis=-1, keepdims=True)
    xc = x - mu
    var = jnp.mean(xc * xc, axis=-1, keepdims=True)
    return xc * lax.rsqrt(var + EPS) * g + b


def _dot(a, b):
    return jnp.dot(a, b, preferred_element_type=F32)


def _swiglu_half_step(x, norm_ref, wg_ref, wu_ref, wd_ref, a_scr):
    d_ff = wg_ref.shape[1]
    h = _rms(x, norm_ref[...]).astype(BF16)
    for c in range(0, d_ff, FF_CHUNK):
        w = min(FF_CHUNK, d_ff - c)
        g = _dot(h, wg_ref[:, c:c + w])
        u = _dot(h, wu_ref[:, c:c + w])
        a_scr[:, c:c + w] = (g * jax.nn.sigmoid(g) * u).astype(BF16)
    return x + 0.5 * _dot(a_scr[...], wd_ref[...])


def _token_kernel(x_ref, n1_ref, wg_ref, wu_ref, wd_ref, nmix_ref, win_ref, lng_ref, lnb_ref,
                  qn_ref, kn_ref, havg_ref,
                  x1_ref, glu_ref, u_ref, vg_ref, q_ref, k_ref, v_ref, ktail_ref, vtail_ref,
                  a_scr, *, tiles_per_seq, d_conv, d_gmlp, d_att):
    x1 = _swiglu_half_step(x_ref[...], n1_ref, wg_ref, wu_ref, wd_ref, a_scr)
    x1_ref[...] = x1
    h = _rms(x1, nmix_ref[...]).astype(BF16)

    o1 = 2 * d_conv
    o2 = o1 + 2 * d_gmlp
    o3 = o2 + d_att
    o4 = o3 + d_att

    zc = _dot(h, win_ref[:, 0:o1])
    glu_ref[...] = zc[:, :d_conv] * jax.nn.sigmoid(zc[:, d_conv:])

    uv = jax.nn.gelu(_dot(h, win_ref[:, o1:o2]))
    u_ref[...] = uv[:, :d_gmlp]
    vg_ref[...] = _layer_norm(uv[:, d_gmlp:], lng_ref[...], lnb_ref[...]).astype(vg_ref.dtype)

    def head_rms(z, g):
        ms = _dot((z * z).astype(BF16), havg_ref[...])
        return z * lax.rsqrt(ms + EPS) * g

    zq = _dot(h, win_ref[:, o2:o3])
    q_ref[...] = head_rms(zq, qn_ref[...] * (HEAD_DIM ** -0.5)).astype(BF16)
    zk = _dot(h, win_ref[:, o3:o4])
    k = head_rms(zk, kn_ref[...])
    k_ref[...] = k.astype(BF16)
    v = _dot(h, win_ref[:, o4:])
    v_ref[...] = v.astype(BF16)

    tail_rows = ktail_ref.shape[0]

    @pl.when(pl.program_id(0) % tiles_per_seq == tiles_per_seq - 1)
    def _():
        ktail_ref[...] = k[k.shape[0] - tail_rows:, :]
        vtail_ref[...] = v[v.shape[0] - tail_rows:, :]


def _resident(stack, l):
    shape = stack.shape[1:]
    return pl.BlockSpec((None,) + shape, lambda *_: (l,) + (0,) * len(shape), pipeline_mode=pl.Buffered(1))


def _token_call(x, p, l, *, seq_len, tail_rows, vg_dtype):
    n_tok, d_model = x.shape
    tm = TOKEN_TILE
    assert n_tok % tm == 0 and (seq_len % tm == 0 or tm % seq_len == 0)
    tiles_per_seq = max(seq_len // tm, 1)
    assert tail_rows <= tm
    n_tail = (n_tok // tm // tiles_per_seq) * tail_rows
    d_ff = p['wg1'].shape[2]
    d_in = p['w_in'].shape[2]
    d_conv, d_gmlp, d_att = p['d_conv'], p['d_gmlp'], p['d_att']

    def rows(width):
        return pl.BlockSpec((tm, width), lambda i: (i, 0))

    tail_spec = pl.BlockSpec((tail_rows, d_att), lambda i: (i // tiles_per_seq, 0))
    params = [p[name] for name in ('n1', 'wg1', 'wu1', 'wd1', 'nmix', 'w_in', 'gm_ln_g', 'gm_ln_b', 'qn', 'kn')]
    in_specs = [rows(d_model)] + [_resident(a, l) for a in params] + [_resident(p['head_avg'], 0)]
    out_shape = [
        jax.ShapeDtypeStruct((n_tok, d_model), F32),
        jax.ShapeDtypeStruct((n_tok, d_conv), F32),
        jax.ShapeDtypeStruct((n_tok, d_gmlp), F32),
        jax.ShapeDtypeStruct((n_tok, d_gmlp), vg_dtype),
        jax.ShapeDtypeStruct((n_tok, d_att), BF16),
        jax.ShapeDtypeStruct((n_tok, d_att), BF16),
        jax.ShapeDtypeStruct((n_tok, d_att), BF16),
        jax.ShapeDtypeStruct((n_tail, d_att), F32),
        jax.ShapeDtypeStruct((n_tail, d_att), F32),
    ]
    out_specs = [rows(d_model), rows(d_conv), rows(d_gmlp), rows(d_gmlp),
                 rows(d_att), rows(d_att), rows(d_att), tail_spec, tail_spec]

    weight_bytes = (3 * _nbytes((d_model, d_ff), BF16) + _nbytes((d_model, d_in), BF16)
                    + _nbytes((d_att, d_att), BF16))
    block_bytes = 2 * (2 * _nbytes((tm, d_model), F32) + 2 * _nbytes((tm, d_conv), F32)
                       + _nbytes((tm, d_gmlp), F32) + 3 * _nbytes((tm, d_att), BF16)
                       + 2 * _nbytes((tail_rows, d_att), F32))
    scratch_bytes = _nbytes((tm, d_ff), BF16)
    temp_bytes = 6 * _nbytes((tm, d_model), F32)

    kernel = functools.partial(_token_kernel, tiles_per_seq=tiles_per_seq,
                               d_conv=d_conv, d_gmlp=d_gmlp, d_att=d_att)
    return pl.pallas_call(
        kernel,
        out_shape=out_shape,
        grid=(n_tok // tm,),
        in_specs=in_specs,
        out_specs=out_specs,
        scratch_shapes=[pltpu.VMEM((tm, d_ff), BF16)],
        compiler_params=pltpu.CompilerParams(
            dimension_semantics=("arbitrary",),
            vmem_limit_bytes=_vmem_limit(weight_bytes + block_bytes, scratch_bytes, temp_bytes)),
        name="token_kernel",
    )(x, *params, p['head_avg'])


def _mixer_kernel(x1_ref, gc_ref, gh_ref, u_ref, vg_ref, q_ref, kc_ref, kh_ref, vc_ref, vh_ref,
                  cw_ref, cb_ref, clg_ref, clb_ref, wcat_ref, gmb_ref, bias_ref,
                  wout_ref, n2_ref, wg_ref, wu_ref, wd_ref,
                  y_ref,
                  gbuf, kk, vv, mixed, a_scr, *, first_tile_has_no_past):
    nb_count, tt, d_model = x1_ref.shape
    d_conv = gc_ref.shape[2]
    d_gmlp = u_ref.shape[2]
    d_att = q_ref.shape[2]
    gm_rows = gmb_ref.shape[0]
    t = pl.program_id(1)
    slab = V7X_MXU_COLS
    n_slabs = d_att // slab

    head_of_lane = lax.broadcasted_iota(jnp.int32, (1, slab), 1) // HEAD_DIM
    gm_head_of_lane = lax.broadcasted_iota(jnp.int32, (1, d_gmlp), 1) // GM_HEAD_DIM
    key_index = lax.broadcasted_iota(jnp.int32, (1, bias_ref.shape[2]), 1)

    for nb in range(nb_count):
        r0 = nb * tt

        hist = gh_ref[nb]
        if first_tile_has_no_past:
            hist = jnp.where(t > 0, hist, 0.0)
        gbuf[0:CONV_HALO, :] = hist
        gbuf[CONV_HALO:, :] = gc_ref[nb]
        first_tap_row = CONV_HALO - CONV_PAD
        acc = None
        for r in range(V7X_SUBLANES):
            n_rows = tt if r == 0 else tt + V7X_SUBLANES
            part = None
            for off in range(r, CONV_HALO + 1, V7X_SUBLANES):
                w = off - first_tap_row
                if 0 <= w < CONV_WIDTH:
                    term = gbuf[off - r:off - r + n_rows, :] * cw_ref[w:w + 1, :]
                    part = term if part is None else part + term
            part = part[r:r + tt, :]
            acc = part if acc is None else acc + part
        yc = _layer_norm(acc + cb_ref[...], clg_ref[...], clb_ref[...])
        mixed[r0:r0 + tt, 0:d_conv] = (yc * jax.nn.sigmoid(yc)).astype(BF16)

        for ci in range(tt // gm_rows):
            rs = ci * gm_rows
            vgc = vg_ref[nb, rs:rs + gm_rows, :].astype(BF16)
            vstack = jnp.concatenate(
                [jnp.where(gm_head_of_lane == hh, vgc, jnp.zeros_like(vgc))
                 for hh in range(d_gmlp // GM_HEAD_DIM)], axis=0)
            s = _dot(wcat_ref[...], vstack) + gmb_ref[...]
            gm = u_ref[nb, rs:rs + gm_rows, :] * s
            mixed[r0 + rs:r0 + rs + gm_rows, d_conv:d_conv + d_gmlp] = gm.astype(BF16)

        kk[0:BAND_PAST, :] = kh_ref[nb]
        kk[BAND_PAST:, :] = kc_ref[nb]
        vv[0:BAND_PAST, :] = vh_ref[nb]
        vv[BAND_PAST:, :] = vc_ref[nb]
        ncb = bias_ref.shape[1] // (HEADS_PER_SLAB * CHUNK)
        band = bias_ref.shape[2]
        for c0 in range(0, tt // CHUNK, ncb):
            cs = c0 * CHUNK
            for g in range(n_slabs):
                ls = g * slab
                qs = []
                for a in range(ncb):
                    q4 = q_ref[nb, cs + a * CHUNK:cs + (a + 1) * CHUNK, ls:ls + slab]
                    qs += [jnp.where(head_of_lane == hl, q4, jnp.zeros_like(q4)) for hl in range(HEADS_PER_SLAB)]
                qs = jnp.concatenate(qs, axis=0)
                kb = kk[cs:cs + band, ls:ls + slab]
                vb = vv[cs:cs + band, ls:ls + slab]
                s = lax.dot_general(qs, kb, (((1,), (1,)), ((), ())), preferred_element_type=F32)
                bias = bias_ref[g]
                if first_tile_has_no_past and c0 < BAND_CHUNKS:
                    first_valid = jnp.where(t == 0, BAND_PAST - cs, 0)
                    bias = jnp.where(key_index >= first_valid, bias, MASK_VALUE)
                s = s + bias
                m = jnp.max(s, axis=-1, keepdims=True)
                e = jnp.exp(s - m)
                denom = jnp.sum(e, axis=-1, keepdims=True)
                o4 = _dot(e.astype(BF16), vb) * (1.0 / denom)
                col = d_conv + d_gmlp + ls
                for a in range(ncb):
                    base = a * HEADS_PER_SLAB * CHUNK
                    o = o4[base:base + CHUNK, :]
                    for hl in range(1, HEADS_PER_SLAB):
                        o = jnp.where(head_of_lane == hl, o4[base + hl * CHUNK:base + (hl + 1) * CHUNK, :], o)
                    rs = r0 + cs + a * CHUNK
                    mixed[rs:rs + CHUNK, col:col + slab] = o.astype(BF16)

    x1 = x1_ref[...].reshape(nb_count * tt, d_model)
    x2 = x1 + _dot(mixed[...], wout_ref[...])
    y = _swiglu_half_step(x2, n2_ref, wg_ref, wu_ref, wd_ref, a_scr)
    y_ref[...] = y.reshape(nb_count, tt, d_model)


def _mixer_call(x1, glu, glu_hist, u, vg, q, k, k_hist, v, v_hist, p, l, tables, *, nb, tt, self_history):
    n_seq, seq_len, d_model = x1.shape
    d_conv, d_gmlp, d_att = p['d_conv'], p['d_gmlp'], p['d_att']
    d_ff = p['wg2'].shape[2]
    assert n_seq % nb == 0 and seq_len % tt == 0 and tt % CHUNK == 0
    wcat, gmb, bias_tab = tables
    gm_rows = wcat.shape[1]
    assert tt % gm_rows == 0
    n_t = seq_len // tt
    rows = nb * tt

    def cur(width):
        return pl.BlockSpec((nb, tt, width), lambda b, t: (b, t, 0))

    if self_history:
        assert tt % BAND_PAST == 0 and glu_hist is None and k_hist is None and v_hist is None
        glu_hist, k_hist, v_hist = glu, k, v
        gh_spec = pl.BlockSpec((nb, CONV_HALO, d_conv),
                               lambda b, t: (b, jnp.maximum(t * (tt // CONV_HALO) - 1, 0), 0))
        past_spec = pl.BlockSpec((nb, BAND_PAST, d_att),
                                 lambda b, t: (b, jnp.maximum(t * (tt // BAND_PAST) - 1, 0), 0))
    else:
        assert n_t == 1
        gh_spec = pl.BlockSpec((None, nb, CONV_HALO, d_conv), lambda b, t: (l, b, 0, 0))
        past_spec = pl.BlockSpec((None, nb, BAND_PAST, d_att), lambda b, t: (l, b, 0, 0))

    params = ([p[name] for name in ('conv_w', 'conv_b', 'conv_ln_g', 'conv_ln_b')] + [wcat, gmb, bias_tab]
              + [p[name] for name in ('w_out', 'n2', 'wg2', 'wu2', 'wd2')])
    in_specs = [
        cur(d_model), cur(d_conv), gh_spec, cur(d_gmlp), cur(d_gmlp), cur(d_att),
        cur(d_att), past_spec, cur(d_att), past_spec,
    ] + [_resident(a, l) for a in params]
    scratch = [
        pltpu.VMEM((CONV_HALO + tt, d_conv), F32),
        pltpu.VMEM((BAND_PAST + tt, d_att), BF16),
        pltpu.VMEM((BAND_PAST + tt, d_att), BF16),
        pltpu.VMEM((rows, d_model), BF16),
        pltpu.VMEM((rows, d_ff), BF16),
    ]
    weight_bytes = (3 * _nbytes((d_model, d_ff), BF16) + _nbytes((d_model, d_model), BF16)
                    + _nbytes(bias_tab.shape[1:], F32)
                    + _nbytes((gm_rows, 4 * gm_rows), BF16) + _nbytes((gm_rows, d_gmlp), F32))
    block_bytes = 2 * (2 * _nbytes((rows, d_model), F32) + 2 * _nbytes((rows, d_conv), F32)
                       + _nbytes((rows, d_gmlp), F32) + 3 * _nbytes((rows, d_att), BF16)
                       + 2 * _nbytes((nb, BAND_PAST, d_att), BF16) + _nbytes((nb, CONV_HALO, d_conv), F32))
    scratch_bytes = (_nbytes((CONV_HALO + tt, d_conv), F32) + 2 * _nbytes((BAND_PAST + tt, d_att), BF16)
                     + _nbytes((rows, d_model), BF16) + _nbytes((rows, d_ff), BF16))
    temp_bytes = 6 * _nbytes((rows, d_model), F32)

    kernel = functools.partial(_mixer_kernel, first_tile_has_no_past=self_history)
    return pl.pallas_call(
        kernel,
        out_shape=jax.ShapeDtypeStruct((n_seq, seq_len, d_model), F32),
        grid=(n_seq // nb, n_t),
        in_specs=in_specs,
        out_specs=cur(d_model),
        scratch_shapes=scratch,
        compiler_params=pltpu.CompilerParams(
            dimension_semantics=("arbitrary", "arbitrary"),
            vmem_limit_bytes=_vmem_limit(weight_bytes + block_bytes, scratch_bytes, temp_bytes)),
        name="mixer_kernel",
    )(x1, glu, glu_hist, u, vg, q, k, k_hist, v, v_hist, *params)


def _rows(v):
    return v[:, None, :].astype(F32)


def _gmlp_tables(ws, bs, rows):
    depth, n_heads = ws.shape[:2]
    w = jnp.tril(ws[:, :, :rows, :rows])
    wcat = jnp.transpose(w, (0, 2, 1, 3)).reshape(depth, rows, n_heads * rows).astype(BF16)
    gmb = jnp.repeat(jnp.swapaxes(bs[:, :, :rows], 1, 2), GM_HEAD_DIM, axis=2).astype(F32)
    return wcat, gmb


def _band_bias(rel_bias, ncb):
    n_heads = rel_bias.shape[0]
    n_q, n_k = ncb * CHUNK, BAND_PAST + ncb * CHUNK
    lo, hi = -(n_q - 1), n_k - 1
    assert -REL_CLIP <= lo and hi >= REL_CLIP
    by_rel = jnp.concatenate(
        [rel_bias[:, lo + REL_CLIP:], jnp.broadcast_to(rel_bias[:, -1:], (n_heads, hi - REL_CLIP))],
        axis=1).astype(F32)
    width = hi - lo + 1
    padded = jnp.pad(by_rel[:, ::-1], ((0, 0), (0, 1)))
    skew = jnp.tile(padded, (1, n_q))[:, :n_q * width].reshape(n_heads, n_q, width)
    tab = skew[:, :, n_q - 1:]
    q_chunk = jnp.arange(n_q)[:, None] // CHUNK
    k_chunk = jnp.arange(n_k)[None, :] // CHUNK
    in_band = (k_chunk >= q_chunk) & (k_chunk <= q_chunk + BAND_CHUNKS)
    tab = jnp.where(in_band, tab, MASK_VALUE)
    tab = tab.reshape(n_heads // HEADS_PER_SLAB, HEADS_PER_SLAB, ncb, CHUNK, n_k)
    return jnp.transpose(tab, (0, 2, 1, 3, 4)).reshape(n_heads // HEADS_PER_SLAB, ncb * HEADS_PER_SLAB * CHUNK, n_k)


def kernel(x_prompt, x_sample, cache_conv, cache_k, cache_v, ffn1_norm, ffn1_w_gate, ffn1_w_up, ffn1_w_down, mix_norm, w_in, conv_w, conv_b, conv_ln_g, conv_ln_b, gmlp_ln_g, gmlp_ln_b, gmlp_ws, gmlp_b, q_norm, k_norm, rel_bias, w_out, ffn2_norm, ffn2_w_gate, ffn2_w_up, ffn2_w_down):
    batch, seq, d_model = x_prompt.shape
    dec_batch, dec_seq, _ = x_sample.shape
    depth = w_in.shape[0]
    d_conv = conv_w.shape[2]
    d_gmlp = gmlp_ln_g.shape[1]
    n_att_heads = rel_bias.shape[1]
    d_att = n_att_heads * HEAD_DIM
    assert cache_k.shape[2] == BAND_PAST and cache_conv.shape[2] == CONV_PAD
    assert dec_seq == CHUNK and dec_seq <= GM_CHUNK

    head_avg = jnp.kron(jnp.eye(n_att_heads, dtype=F32),
                        jnp.full((HEAD_DIM, HEAD_DIM), 1.0 / HEAD_DIM, F32)).astype(BF16)[None]
    p = dict(
        d_conv=d_conv, d_gmlp=d_gmlp, d_att=d_att,
        n1=_rows(ffn1_norm), wg1=ffn1_w_gate.astype(BF16), wu1=ffn1_w_up.astype(BF16),
        wd1=ffn1_w_down.astype(BF16),
        nmix=_rows(mix_norm), w_in=w_in.astype(BF16),
        gm_ln_g=_rows(gmlp_ln_g), gm_ln_b=_rows(gmlp_ln_b),
        qn=_rows(jnp.tile(q_norm, (1, n_att_heads))), kn=_rows(jnp.tile(k_norm, (1, n_att_heads))),
        head_avg=head_avg,
        conv_w=conv_w.astype(F32), conv_b=_rows(conv_b),
        conv_ln_g=_rows(conv_ln_g), conv_ln_b=_rows(conv_ln_b),
        w_out=w_out.astype(BF16), n2=_rows(ffn2_norm), wg2=ffn2_w_gate.astype(BF16),
        wu2=ffn2_w_up.astype(BF16), wd2=ffn2_w_down.astype(BF16),
    )

    def band_bias(ncb):
        tab = _band_bias(rel_bias.reshape(depth * n_att_heads, -1), ncb)
        return tab.reshape((depth, n_att_heads // HEADS_PER_SLAB) + tab.shape[1:])

    prompt_tables = _gmlp_tables(gmlp_ws, gmlp_b, GM_CHUNK) + (band_bias(ATT_CHUNKS_PER_DOT),)
    sample_tables = _gmlp_tables(gmlp_ws, gmlp_b, dec_seq) + (band_bias(1),)
    conv_hist = jnp.pad(cache_conv.astype(F32), ((0, 0), (0, 0), (CONV_HALO - CONV_PAD, 0), (0, 0)))
    k_hist = cache_k.reshape(depth, dec_batch, BAND_PAST, d_att).astype(BF16)
    v_hist = cache_v.reshape(depth, dec_batch, BAND_PAST, d_att).astype(BF16)

    xp = x_prompt.reshape(batch * seq, d_model)
    xs = x_sample.reshape(dec_batch * dec_seq, d_model)
    outs = {name: [] for name in ('p_conv', 'p_k', 'p_v', 's_conv', 's_k', 's_v', 's_gv')}
    sample_streams_per_step = TOKEN_TILE // dec_seq // 2

    for l in range(depth):
        x1, glu, u, vg, q, k, v, k_tail, v_tail = _token_call(
            xp, p, l, seq_len=seq, tail_rows=min(BAND_PAST, seq), vg_dtype=BF16)
        seq3 = lambda a: a.reshape(batch, seq, a.shape[-1])
        xp = _mixer_call(seq3(x1), seq3(glu), None, seq3(u), seq3(vg), seq3(q), seq3(k), None, seq3(v), None,
                         p, l, prompt_tables, nb=1, tt=TOKEN_TILE, self_history=True).reshape(batch * seq, d_model)
        outs['p_conv'].append(seq3(glu)[:, seq - CONV_PAD:])
        outs['p_k'].append(k_tail.reshape(batch, -1, n_att_heads, HEAD_DIM))
        outs['p_v'].append(v_tail.reshape(batch, -1, n_att_heads, HEAD_DIM))

        x1, glu, u, vg, q, k, v, k_new, v_new = _token_call(
            xs, p, l, seq_len=dec_seq, tail_rows=TOKEN_TILE, vg_dtype=F32)
        dec3 = lambda a: a.reshape(dec_batch, dec_seq, a.shape[-1])
        xs = _mixer_call(dec3(x1), dec3(glu), conv_hist, dec3(u), dec3(vg), dec3(q), dec3(k), k_hist,
                         dec3(v), v_hist, p, l, sample_tables, nb=sample_streams_per_step, tt=dec_seq,
                         self_history=False).reshape(dec_batch * dec_seq, d_model)
        outs['s_conv'].append(dec3(glu)[:, dec_seq - CONV_PAD:])
        outs['s_k'].append(k_new.reshape(dec_batch, dec_seq, n_att_heads, HEAD_DIM))
        outs['s_v'].append(v_new.reshape(dec_batch, dec_seq, n_att_heads, HEAD_DIM))
        outs['s_gv'].append(vg.reshape(dec_batch, dec_seq, d_gmlp // GM_HEAD_DIM, GM_HEAD_DIM))

    return (xp.reshape(batch, seq, d_model), xs.reshape(dec_batch, dec_seq, d_model),
            jnp.stack(outs['p_conv']), jnp.stack(outs['p_k']), jnp.stack(outs['p_v']),
            jnp.stack(outs['s_conv']), jnp.stack(outs['s_k']), jnp.stack(outs['s_v']),
            jnp.stack(outs['s_gv']))
```

```python
import functools

import jax
import jax.numpy as jnp
from jax import lax
from jax.experimental import pallas as pl
from jax.experimental.pallas import tpu as pltpu

F32 = jnp.float32
BF16 = jnp.bfloat16

EPS = 1e-6
CHUNK = 64
BAND_CHUNKS = 8
BAND_PAST = BAND_CHUNKS * CHUNK
BAND_LEN = BAND_PAST + CHUNK
REL_CLIP = 128
CONV_WIDTH = 31
CONV_PAD = CONV_WIDTH - 1
CONV_HALO = 32
HEAD_DIM = 64
GM_HEAD_DIM = 64
GM_CHUNK = 128
MASK_VALUE = -1e30

V7X_LANES = 128
V7X_SUBLANES = 8
V7X_MXU_COLS = 256
V7X_VMEM_BYTES = 64 * 1024 * 1024
HEADS_PER_SLAB = V7X_MXU_COLS // HEAD_DIM

TOKEN_TILE = 512
FF_CHUNK = 256
ATT_CHUNKS_PER_DOT = 1


def _vmem_limit(block_bytes, scratch_bytes, temp_bytes):
    return int(min(V7X_VMEM_BYTES - (4 << 20), block_bytes + scratch_bytes + temp_bytes))


def _nbytes(shape, dtype):
    n = 1
    for s in shape:
        n *= s
    return n * jnp.dtype(dtype).itemsize


def _rms(x, g):
    return x * lax.rsqrt(jnp.mean(x * x, axis=-1, keepdims=True) + EPS) * g


def _layer_norm(x, g, b):
    mu = jnp.mean(x, axis=-1, keepdims=True)
    xc = x - mu
    var = jnp.mean(xc * xc, axis=-1, keepdims=True)
    return xc * lax.rsqrt(var + EPS) * g + b


def _dot(a, b):
    return jnp.dot(a, b, preferred_element_type=F32)


def _swiglu_half_step(x, norm_ref, wg_ref, wu_ref, wd_ref, a_scr):
    d_ff = wg_ref.shape[1]
    h = _rms(x, norm_ref[...]).astype(BF16)
    for c in range(0, d_ff, FF_CHUNK):
        w = min(FF_CHUNK, d_ff - c)
        g = _dot(h, wg_ref[:, c:c + w])
        u = _dot(h, wu_ref[:, c:c + w])
        a_scr[:, c:c + w] = (g * jax.nn.sigmoid(g) * u).astype(BF16)
    return x + 0.5 * _dot(a_scr[...], wd_ref[...])


def _token_kernel(x_ref, n1_ref, wg_ref, wu_ref, wd_ref, nmix_ref, win_ref, lng_ref, lnb_ref,
                  qn_ref, kn_ref, havg_ref,
                  x1_ref, glu_ref, u_ref, vg_ref, q_ref, k_ref, v_ref, ktail_ref, vtail_ref,
                  a_scr, *, tiles_per_seq, d_conv, d_gmlp, d_att):
    x1 = _swiglu_half_step(x_ref[...], n1_ref, wg_ref, wu_ref, wd_ref, a_scr)
    x1_ref[...] = x1
    h = _rms(x1, nmix_ref[...]).astype(BF16)

    o1 = 2 * d_conv
    o2 = o1 + 2 * d_gmlp
    o3 = o2 + d_att
    o4 = o3 + d_att

    zc = _dot(h, win_ref[:, 0:o1])
    glu_ref[...] = zc[:, :d_conv] * jax.nn.sigmoid(zc[:, d_conv:])

    uv = jax.nn.gelu(_dot(h, win_ref[:, o1:o2]))
    u_ref[...] = uv[:, :d_gmlp]
    vg_ref[...] = _layer_norm(uv[:, d_gmlp:], lng_ref[...], lnb_ref[...]).astype(vg_ref.dtype)

    def head_rms(z, g):
        sq = (z * z).astype(BF16)
        slab = havg_ref.shape[0]
        ms = jnp.concatenate([_dot(sq[:, c:c + slab], havg_ref[...]) for c in range(0, z.shape[1], slab)], axis=1)
        return z * lax.rsqrt(ms + EPS) * g

    zq = _dot(h, win_ref[:, o2:o3])
    q_ref[...] = head_rms(zq, qn_ref[...] * (HEAD_DIM ** -0.5)).astype(BF16)
    zk = _dot(h, win_ref[:, o3:o4])
    k = head_rms(zk, kn_ref[...])
    k_ref[...] = k.astype(BF16)
    v = _dot(h, win_ref[:, o4:])
    v_ref[...] = v.astype(BF16)

    tail_rows = ktail_ref.shape[0]

    @pl.when(pl.program_id(0) % tiles_per_seq == tiles_per_seq - 1)
    def _():
        ktail_ref[...] = k[k.shape[0] - tail_rows:, :]
        vtail_ref[...] = v[v.shape[0] - tail_rows:, :]


def _resident(stack, l):
    shape = stack.shape[1:]
    return pl.BlockSpec((None,) + shape, lambda *_: (l,) + (0,) * len(shape), pipeline_mode=pl.Buffered(1))


def _token_call(x, p, l, *, seq_len, tail_rows, vg_dtype):
    n_tok, d_model = x.shape
    tm = TOKEN_TILE
    assert n_tok % tm == 0 and (seq_len % tm == 0 or tm % seq_len == 0)
    tiles_per_seq = max(seq_len // tm, 1)
    assert tail_rows <= tm
    n_tail = (n_tok // tm // tiles_per_seq) * tail_rows
    d_ff = p['wg1'].shape[2]
    d_in = p['w_in'].shape[2]
    d_conv, d_gmlp, d_att = p['d_conv'], p['d_gmlp'], p['d_att']

    def rows(width):
        return pl.BlockSpec((tm, width), lambda i: (i, 0))

    tail_spec = pl.BlockSpec((tail_rows, d_att), lambda i: (i // tiles_per_seq, 0))
    params = [p[name] for name in ('n1', 'wg1', 'wu1', 'wd1', 'nmix', 'w_in', 'gm_ln_g', 'gm_ln_b', 'qn', 'kn')]
    in_specs = [rows(d_model)] + [_resident(a, l) for a in params] + [_resident(p['head_avg'], 0)]
    out_shape = [
        jax.ShapeDtypeStruct((n_tok, d_model), F32),
        jax.ShapeDtypeStruct((n_tok, d_conv), F32),
        jax.ShapeDtypeStruct((n_tok, d_gmlp), F32),
        jax.ShapeDtypeStruct((n_tok, d_gmlp), vg_dtype),
        jax.ShapeDtypeStruct((n_tok, d_att), BF16),
        jax.ShapeDtypeStruct((n_tok, d_att), BF16),
        jax.ShapeDtypeStruct((n_tok, d_att), BF16),
        jax.ShapeDtypeStruct((n_tail, d_att), F32),
        jax.ShapeDtypeStruct((n_tail, d_att), F32),
    ]
    out_specs = [rows(d_model), rows(d_conv), rows(d_gmlp), rows(d_gmlp),
                 rows(d_att), rows(d_att), rows(d_att), tail_spec, tail_spec]

    weight_bytes = (3 * _nbytes((d_model, d_ff), BF16) + _nbytes((d_model, d_in), BF16)
                    + _nbytes(p['head_avg'].shape[1:], BF16))
    block_bytes = 2 * (2 * _nbytes((tm, d_model), F32) + 2 * _nbytes((tm, d_conv), F32)
                       + _nbytes((tm, d_gmlp), F32) + 3 * _nbytes((tm, d_att), BF16)
                       + 2 * _nbytes((tail_rows, d_att), F32))
    scratch_bytes = _nbytes((tm, d_ff), BF16)
    temp_bytes = 6 * _nbytes((tm, d_model), F32)

    kernel = functools.partial(_token_kernel, tiles_per_seq=tiles_per_seq,
                               d_conv=d_conv, d_gmlp=d_gmlp, d_att=d_att)
    return pl.pallas_call(
        kernel,
        out_shape=out_shape,
        grid=(n_tok // tm,),
        in_specs=in_specs,
        out_specs=out_specs,
        scratch_shapes=[pltpu.VMEM((tm, d_ff), BF16)],
        compiler_params=pltpu.CompilerParams(
            dimension_semantics=("arbitrary",),
            vmem_limit_bytes=_vmem_limit(weight_bytes + block_bytes, scratch_bytes, temp_bytes)),
        name="token_kernel",
    )(x, *params, p['head_avg'])


def _mixer_kernel(x1_ref, gc_ref, gh_ref, u_ref, vg_ref, q_ref, kc_ref, kh_ref, vc_ref, vh_ref,
                  cw_ref, cb_ref, clg_ref, clb_ref, wcat_ref, gmb_ref, bias_ref,
                  wout_ref, n2_ref, wg_ref, wu_ref, wd_ref,
                  y_ref,
                  gbuf, kk, vv, mixed, a_scr, *, first_tile_has_no_past):
    nb_count, tt, d_model = x1_ref.shape
    d_conv = gc_ref.shape[2]
    d_gmlp = u_ref.shape[2]
    d_att = q_ref.shape[2]
    gm_rows = gmb_ref.shape[0]
    t = pl.program_id(1)
    slab = V7X_MXU_COLS
    n_slabs = d_att // slab

    head_of_lane = lax.broadcasted_iota(jnp.int32, (1, slab), 1) // HEAD_DIM
    gm_head_of_lane = lax.broadcasted_iota(jnp.int32, (1, d_gmlp), 1) // GM_HEAD_DIM
    key_index = lax.broadcasted_iota(jnp.int32, (1, bias_ref.shape[2]), 1)

    for nb in range(nb_count):
        r0 = nb * tt

        hist = gh_ref[nb]
        if first_tile_has_no_past:
            hist = jnp.where(t > 0, hist, 0.0)
        gbuf[0:CONV_HALO, :] = hist
        gbuf[CONV_HALO:, :] = gc_ref[nb]
        first_tap_row = CONV_HALO - CONV_PAD
        acc = None
        for r in range(V7X_SUBLANES):
            n_rows = tt if r == 0 else tt + V7X_SUBLANES
            part = None
            for off in range(r, CONV_HALO + 1, V7X_SUBLANES):
                w = off - first_tap_row
                if 0 <= w < CONV_WIDTH:
                    term = gbuf[off - r:off - r + n_rows, :] * cw_ref[w:w + 1, :]
                    part = term if part is None else part + term
            part = part[r:r + tt, :]
            acc = part if acc is None else acc + part
        yc = _layer_norm(acc + cb_ref[...], clg_ref[...], clb_ref[...])
        mixed[r0:r0 + tt, 0:d_conv] = (yc * jax.nn.sigmoid(yc)).astype(BF16)

        for ci in range(tt // gm_rows):
            rs = ci * gm_rows
            vgc = vg_ref[nb, rs:rs + gm_rows, :].astype(BF16)
            vstack = jnp.concatenate(
                [jnp.where(gm_head_of_lane == hh, vgc, jnp.zeros_like(vgc))
                 for hh in range(d_gmlp // GM_HEAD_DIM)], axis=0)
            s = _dot(wcat_ref[...], vstack) + gmb_ref[...]
            gm = u_ref[nb, rs:rs + gm_rows, :] * s
            mixed[r0 + rs:r0 + rs + gm_rows, d_conv:d_conv + d_gmlp] = gm.astype(BF16)

        kk[0:BAND_PAST, :] = kh_ref[nb]
        kk[BAND_PAST:, :] = kc_ref[nb]
        vv[0:BAND_PAST, :] = vh_ref[nb]
        vv[BAND_PAST:, :] = vc_ref[nb]
        ncb = bias_ref.shape[1] // (HEADS_PER_SLAB * CHUNK)
        band = bias_ref.shape[2]
        for c0 in range(0, tt // CHUNK, ncb):
            cs = c0 * CHUNK
            for g in range(n_slabs):
                ls = g * slab
                qs = []
                for a in range(ncb):
                    q4 = q_ref[nb, cs + a * CHUNK:cs + (a + 1) * CHUNK, ls:ls + slab]
                    qs += [jnp.where(head_of_lane == hl, q4, jnp.zeros_like(q4)) for hl in range(HEADS_PER_SLAB)]
                qs = jnp.concatenate(qs, axis=0)
                kb = kk[cs:cs + band, ls:ls + slab]
                vb = vv[cs:cs + band, ls:ls + slab]
                s = lax.dot_general(qs, kb, (((1,), (1,)), ((), ())), preferred_element_type=F32)
                bias = bias_ref[g]
                if first_tile_has_no_past and c0 < BAND_CHUNKS:
                    first_valid = jnp.where(t == 0, BAND_PAST - cs, 0)
                    bias = jnp.where(key_index >= first_valid, bias, MASK_VALUE)
                s = s + bias
                m = jnp.max(s, axis=-1, keepdims=True)
                e = jnp.exp(s - m)
                denom = jnp.sum(e, axis=-1, keepdims=True)
                o4 = _dot(e.astype(BF16), vb) * (1.0 / denom)
                col = d_conv + d_gmlp + ls
                for a in range(ncb):
                    base = a * HEADS_PER_SLAB * CHUNK
                    o = o4[base:base + CHUNK, :]
                    for hl in range(1, HEADS_PER_SLAB):
                        o = jnp.where(head_of_lane == hl, o4[base + hl * CHUNK:base + (hl + 1) * CHUNK, :], o)
                    rs = r0 + cs + a * CHUNK
                    mixed[rs:rs + CHUNK, col:col + slab] = o.astype(BF16)

    x1 = x1_ref[...].reshape(nb_count * tt, d_model)
    d_cg = d_conv + d_gmlp
    x2 = x1 + _dot(mixed[:, d_cg:], wout_ref[d_cg:, :])
    x2 = x2 + _dot(mixed[:, 0:d_cg], wout_ref[0:d_cg, :])
    y = _swiglu_half_step(x2, n2_ref, wg_ref, wu_ref, wd_ref, a_scr)
    y_ref[...] = y.reshape(nb_count, tt, d_model)


def _mixer_call(x1, glu, glu_hist, u, vg, q, k, k_hist, v, v_hist, p, l, tables, *, nb, tt, self_history):
    n_seq, seq_len, d_model = x1.shape
    d_conv, d_gmlp, d_att = p['d_conv'], p['d_gmlp'], p['d_att']
    d_ff = p['wg2'].shape[2]
    assert n_seq % nb == 0 and seq_len % tt == 0 and tt % CHUNK == 0
    wcat, gmb, bias_tab = tables
    gm_rows = wcat.shape[1]
    assert tt % gm_rows == 0
    n_t = seq_len // tt
    rows = nb * tt

    def cur(width):
        return pl.BlockSpec((nb, tt, width), lambda b, t: (b, t, 0))

    if self_history:
        assert tt % BAND_PAST == 0 and glu_hist is None and k_hist is None and v_hist is None
        glu_hist, k_hist, v_hist = glu, k, v
        gh_spec = pl.BlockSpec((nb, CONV_HALO, d_conv),
                               lambda b, t: (b, jnp.maximum(t * (tt // CONV_HALO) - 1, 0), 0))
        past_spec = pl.BlockSpec((nb, BAND_PAST, d_att),
                                 lambda b, t: (b, jnp.maximum(t * (tt // BAND_PAST) - 1, 0), 0))
    else:
        assert n_t == 1
        gh_spec = pl.BlockSpec((None, nb, CONV_HALO, d_conv), lambda b, t: (l, b, 0, 0))
        past_spec = pl.BlockSpec((None, nb, BAND_PAST, d_att), lambda b, t: (l, b, 0, 0))

    params = ([p[name] for name in ('conv_w', 'conv_b', 'conv_ln_g', 'conv_ln_b')] + [wcat, gmb, bias_tab]
              + [p[name] for name in ('w_out', 'n2', 'wg2', 'wu2', 'wd2')])
    in_specs = [
        cur(d_model), cur(d_conv), gh_spec, cur(d_gmlp), cur(d_gmlp), cur(d_att),
        cur(d_att), past_spec, cur(d_att), past_spec,
    ] + [_resident(a, l) for a in params]
    scratch = [
        pltpu.VMEM((CONV_HALO + tt, d_conv), F32),
        pltpu.VMEM((BAND_PAST + tt, d_att), BF16),
        pltpu.VMEM((BAND_PAST + tt, d_att), BF16),
        pltpu.VMEM((rows, d_model), BF16),
        pltpu.VMEM((rows, d_ff), BF16),
    ]
    weight_bytes = (3 * _nbytes((d_model, d_ff), BF16) + _nbytes((d_model, d_model), BF16)
                    + _nbytes(bias_tab.shape[1:], F32)
                    + _nbytes((gm_rows, 4 * gm_rows), BF16) + _nbytes((gm_rows, d_gmlp), F32))
    block_bytes = 2 * (2 * _nbytes((rows, d_model), F32) + 2 * _nbytes((rows, d_conv), F32)
                       + _nbytes((rows, d_gmlp), F32) + 3 * _nbytes((rows, d_att), BF16)
                       + 2 * _nbytes((nb, BAND_PAST, d_att), BF16) + _nbytes((nb, CONV_HALO, d_conv), F32))
    scratch_bytes = (_nbytes((CONV_HALO + tt, d_conv), F32) + 2 * _nbytes((BAND_PAST + tt, d_att), BF16)
                     + _nbytes((rows, d_model), BF16) + _nbytes((rows, d_ff), BF16))
    temp_bytes = 6 * _nbytes((rows, d_model), F32)

    kernel = functools.partial(_mixer_kernel, first_tile_has_no_past=self_history)
    return pl.pallas_call(
        kernel,
        out_shape=jax.ShapeDtypeStruct((n_seq, seq_len, d_model), F32),
        grid=(n_seq // nb, n_t),
        in_specs=in_specs,
        out_specs=cur(d_model),
        scratch_shapes=scratch,
        compiler_params=pltpu.CompilerParams(
            dimension_semantics=("arbitrary", "arbitrary"),
            vmem_limit_bytes=_vmem_limit(weight_bytes + block_bytes, scratch_bytes, temp_bytes)),
        name="mixer_kernel",
    )(x1, glu, glu_hist, u, vg, q, k, k_hist, v, v_hist, *params)


def _rows(v):
    return v[:, None, :].astype(F32)


def _gmlp_tables(ws, bs, rows):
    depth, n_heads = ws.shape[:2]
    w = jnp.tril(ws[:, :, :rows, :rows])
    wcat = jnp.transpose(w, (0, 2, 1, 3)).reshape(depth, rows, n_heads * rows).astype(BF16)
    gmb = jnp.repeat(jnp.swapaxes(bs[:, :, :rows], 1, 2), GM_HEAD_DIM, axis=2).astype(F32)
    return wcat, gmb


def _band_bias(rel_bias, ncb):
    n_heads = rel_bias.shape[0]
    n_q, n_k = ncb * CHUNK, BAND_PAST + ncb * CHUNK
    lo, hi = -(n_q - 1), n_k - 1
    assert -REL_CLIP <= lo and hi >= REL_CLIP
    by_rel = jnp.concatenate(
        [rel_bias[:, lo + REL_CLIP:], jnp.broadcast_to(rel_bias[:, -1:], (n_heads, hi - REL_CLIP))],
        axis=1).astype(F32)
    width = hi - lo + 1
    padded = jnp.pad(by_rel[:, ::-1], ((0, 0), (0, 1)))
    skew = jnp.tile(padded, (1, n_q))[:, :n_q * width].reshape(n_heads, n_q, width)
    tab = skew[:, :, n_q - 1:]
    q_chunk = jnp.arange(n_q)[:, None] // CHUNK
    k_chunk = jnp.arange(n_k)[None, :] // CHUNK
    in_band = (k_chunk >= q_chunk) & (k_chunk <= q_chunk + BAND_CHUNKS)
    tab = jnp.where(in_band, tab, MASK_VALUE)
    tab = tab.reshape(n_heads // HEADS_PER_SLAB, HEADS_PER_SLAB, ncb, CHUNK, n_k)
    return jnp.transpose(tab, (0, 2, 1, 3, 4)).reshape(n_heads // HEADS_PER_SLAB, ncb * HEADS_PER_SLAB * CHUNK, n_k)


def kernel(x_prompt, x_sample, cache_conv, cache_k, cache_v, ffn1_norm, ffn1_w_gate, ffn1_w_up, ffn1_w_down, mix_norm, w_in, conv_w, conv_b, conv_ln_g, conv_ln_b, gmlp_ln_g, gmlp_ln_b, gmlp_ws, gmlp_b, q_norm, k_norm, rel_bias, w_out, ffn2_norm, ffn2_w_gate, ffn2_w_up, ffn2_w_down):
    batch, seq, d_model = x_prompt.shape
    dec_batch, dec_seq, _ = x_sample.shape
    depth = w_in.shape[0]
    d_conv = conv_w.shape[2]
    d_gmlp = gmlp_ln_g.shape[1]
    n_att_heads = rel_bias.shape[1]
    d_att = n_att_heads * HEAD_DIM
    assert cache_k.shape[2] == BAND_PAST and cache_conv.shape[2] == CONV_PAD
    assert dec_seq == CHUNK and dec_seq <= GM_CHUNK

    head_avg = jnp.kron(jnp.eye(HEADS_PER_SLAB, dtype=F32),
                        jnp.full((HEAD_DIM, HEAD_DIM), 1.0 / HEAD_DIM, F32)).astype(BF16)[None]
    p = dict(
        d_conv=d_conv, d_gmlp=d_gmlp, d_att=d_att,
        n1=_rows(ffn1_norm), wg1=ffn1_w_gate.astype(BF16), wu1=ffn1_w_up.astype(BF16),
        wd1=ffn1_w_down.astype(BF16),
        nmix=_rows(mix_norm), w_in=w_in.astype(BF16),
        gm_ln_g=_rows(gmlp_ln_g), gm_ln_b=_rows(gmlp_ln_b),
        qn=_rows(jnp.tile(q_norm, (1, n_att_heads))), kn=_rows(jnp.tile(k_norm, (1, n_att_heads))),
        head_avg=head_avg,
        conv_w=conv_w.astype(F32), conv_b=_rows(conv_b),
        conv_ln_g=_rows(conv_ln_g), conv_ln_b=_rows(conv_ln_b),
        w_out=w_out.astype(BF16), n2=_rows(ffn2_norm), wg2=ffn2_w_gate.astype(BF16),
        wu2=ffn2_w_up.astype(BF16), wd2=ffn2_w_down.astype(BF16),
    )

    def band_bias(ncb):
        tab = _band_bias(rel_bias.reshape(depth * n_att_heads, -1), ncb)
        return tab.reshape((depth, n_att_heads // HEADS_PER_SLAB) + tab.shape[1:])

    prompt_tables = _gmlp_tables(gmlp_ws, gmlp_b, GM_CHUNK) + (band_bias(ATT_CHUNKS_PER_DOT),)
    sample_tables = _gmlp_tables(gmlp_ws, gmlp_b, dec_seq) + (band_bias(1),)
    conv_hist = jnp.pad(cache_conv.astype(F32), ((0, 0), (0, 0), (CONV_HALO - CONV_PAD, 0), (0, 0)))
    k_hist = cache_k.reshape(depth, dec_batch, BAND_PAST, d_att).astype(BF16)
    v_hist = cache_v.reshape(depth, dec_batch, BAND_PAST, d_att).astype(BF16)

    xp = x_prompt.reshape(batch * seq, d_model)
    xs = x_sample.reshape(dec_batch * dec_seq, d_model)
    outs = {name: [] for name in ('p_conv', 'p_k', 'p_v', 's_conv', 's_k', 's_v', 's_gv')}
    sample_streams_per_step = TOKEN_TILE // dec_seq // 2

    for l in range(depth):
        x1, glu, u, vg, q, k, v, k_tail, v_tail = _token_call(
            xp, p, l, seq_len=seq, tail_rows=min(BAND_PAST, seq), vg_dtype=BF16)
        seq3 = lambda a: a.reshape(batch, seq, a.shape[-1])
        xp = _mixer_call(seq3(x1), seq3(glu), None, seq3(u), seq3(vg), seq3(q), seq3(k), None, seq3(v), None,
                         p, l, prompt_tables, nb=1, tt=TOKEN_TILE, self_history=True).reshape(batch * seq, d_model)
        outs['p_conv'].append(seq3(glu)[:, seq - CONV_PAD:])
        outs['p_k'].append(k_tail.reshape(batch, -1, n_att_heads, HEAD_DIM))
        outs['p_v'].append(v_tail.reshape(batch, -1, n_att_heads, HEAD_DIM))

        x1, glu, u, vg, q, k, v, k_new, v_new = _token_call(
            xs, p, l, seq_len=dec_seq, tail_rows=TOKEN_TILE, vg_dtype=F32)
        dec3 = lambda a: a.reshape(dec_batch, dec_seq, a.shape[-1])
        xs = _mixer_call(dec3(x1), dec3(glu), conv_hist, dec3(u), dec3(vg), dec3(q), dec3(k), k_hist,
                         dec3(v), v_hist, p, l, sample_tables, nb=sample_streams_per_step, tt=dec_seq,
                         self_history=False).reshape(dec_batch * dec_seq, d_model)
        outs['s_conv'].append(dec3(glu)[:, dec_seq - CONV_PAD:])
        outs['s_k'].append(k_new.reshape(dec_batch, dec_seq, n_att_heads, HEAD_DIM))
        outs['s_v'].append(v_new.reshape(dec_batch, dec_seq, n_att_heads, HEAD_DIM))
        outs['s_gv'].append(vg.reshape(dec_batch, dec_seq, d_gmlp // GM_HEAD_DIM, GM_HEAD_DIM))

    return (xp.reshape(batch, seq, d_model), xs.reshape(dec_batch, dec_seq, d_model),
            jnp.stack(outs['p_conv']), jnp.stack(outs['p_k']), jnp.stack(outs['p_v']),
            jnp.stack(outs['s_conv']), jnp.stack(outs['s_k']), jnp.stack(outs['s_v']),
            jnp.stack(outs['s_gv']))
```

```python
import functools

import jax
import jax.numpy as jnp
from jax import lax
from jax.experimental import pallas as pl
from jax.experimental.pallas import tpu as pltpu

F32 = jnp.float32
BF16 = jnp.bfloat16

EPS = 1e-6
CHUNK = 64
BAND_CHUNKS = 8
BAND_PAST = BAND_CHUNKS * CHUNK
BAND_LEN = BAND_PAST + CHUNK
REL_CLIP = 128
CONV_WIDTH = 31
CONV_PAD = CONV_WIDTH - 1
CONV_HALO = 32
HEAD_DIM = 64
GM_HEAD_DIM = 64
GM_CHUNK = 128
MASK_VALUE = -1e30

V7X_LANES = 128
V7X_SUBLANES = 8
V7X_MXU_COLS = 256
V7X_VMEM_BYTES = 64 * 1024 * 1024
HEADS_PER_SLAB = V7X_MXU_COLS // HEAD_DIM

TOKEN_TILE = 512
FF_CHUNK = 256


def _vmem_limit(block_bytes, scratch_bytes, temp_bytes):
    return int(min(V7X_VMEM_BYTES - (4 << 20), block_bytes + scratch_bytes + temp_bytes))


def _nbytes(shape, dtype):
    n = 1
    for s in shape:
        n *= s
    return n * jnp.dtype(dtype).itemsize


def _rms_factors(x, g):
    return (x * g).astype(BF16), lax.rsqrt(jnp.mean(x * x, axis=-1, keepdims=True) + EPS)


def _layer_norm(x, g, b):
    mu = jnp.mean(x, axis=-1, keepdims=True)
    xc = x - mu
    var = jnp.mean(xc * xc, axis=-1, keepdims=True)
    return xc * lax.rsqrt(var + EPS) * g + b


def _dot(a, b):
    return jnp.dot(a, b, preferred_element_type=F32)


def _swiglu_half_step(x, norm_ref, wg_ref, wu_ref, wd_ref, a_scr):
    d_ff = wg_ref.shape[1]
    xg, r = _rms_factors(x, norm_ref[...])
    for c in range(0, d_ff, FF_CHUNK):
        w = min(FF_CHUNK, d_ff - c)
        g = r * _dot(xg, wg_ref[:, c:c + w])
        u = r * _dot(xg, wu_ref[:, c:c + w])
        a_scr[:, c:c + w] = (g * jax.nn.sigmoid(g) * u).astype(BF16)
    return x + 0.5 * _dot(a_scr[...], wd_ref[...])


def _token_kernel(x_ref, n1_ref, wg_ref, wu_ref, wd_ref, nmix_ref, win_ref, lng_ref, lnb_ref,
                  qn_ref, kn_ref, havg_ref,
                  x1_ref, glu_ref, u_ref, vg_ref, q_ref, k_ref, v_ref, ktail_ref, vtail_ref,
                  a_scr, *, tiles_per_seq, d_conv, d_gmlp, d_att):
    x1 = _swiglu_half_step(x_ref[...], n1_ref, wg_ref, wu_ref, wd_ref, a_scr)
    x1_ref[...] = x1
    h, r = _rms_factors(x1, nmix_ref[...])

    o1 = 2 * d_conv
    o2 = o1 + 2 * d_gmlp
    o3 = o2 + d_att
    o4 = o3 + d_att

    zc = r * _dot(h, win_ref[:, 0:o1])
    glu_ref[...] = zc[:, :d_conv] * jax.nn.sigmoid(zc[:, d_conv:])

    uv = jax.nn.gelu(r * _dot(h, win_ref[:, o1:o2]))
    u_ref[...] = uv[:, :d_gmlp]
    vg_ref[...] = _layer_norm(uv[:, d_gmlp:], lng_ref[...], lnb_ref[...]).astype(vg_ref.dtype)

    def head_rms(z, g):
        sq = (z * z).astype(BF16)
        slab = havg_ref.shape[0]
        ms = jnp.concatenate([_dot(sq[:, c:c + slab], havg_ref[...]) for c in range(0, z.shape[1], slab)], axis=1)
        return z * lax.rsqrt(ms + EPS) * g

    zq = r * _dot(h, win_ref[:, o2:o3])
    q_ref[...] = head_rms(zq, qn_ref[...] * (HEAD_DIM ** -0.5)).astype(BF16)
    zk = r * _dot(h, win_ref[:, o3:o4])
    k = head_rms(zk, kn_ref[...])
    k_ref[...] = k.astype(BF16)
    v = r * _dot(h, win_ref[:, o4:])
    v_ref[...] = v.astype(BF16)

    tail_rows = ktail_ref.shape[0]

    @pl.when(pl.program_id(0) % tiles_per_seq == tiles_per_seq - 1)
    def _():
        ktail_ref[...] = k[k.shape[0] - tail_rows:, :]
        vtail_ref[...] = v[v.shape[0] - tail_rows:, :]


def _resident(stack, l):
    shape = stack.shape[1:]
    return pl.BlockSpec((None,) + shape, lambda *_: (l,) + (0,) * len(shape), pipeline_mode=pl.Buffered(1))


def _token_call(x, p, l, *, seq_len, tail_rows, vg_dtype):
    n_tok, d_model = x.shape
    tm = TOKEN_TILE
    assert n_tok % tm == 0 and (seq_len % tm == 0 or tm % seq_len == 0)
    tiles_per_seq = max(seq_len // tm, 1)
    assert tail_rows <= tm
    n_tail = (n_tok // tm // tiles_per_seq) * tail_rows
    d_ff = p['wg1'].shape[2]
    d_in = p['w_in'].shape[2]
    d_conv, d_gmlp, d_att = p['d_conv'], p['d_gmlp'], p['d_att']

    def rows(width):
        return pl.BlockSpec((tm, width), lambda i: (i, 0))

    tail_spec = pl.BlockSpec((tail_rows, d_att), lambda i: (i // tiles_per_seq, 0))
    params = [p[name] for name in ('n1', 'wg1', 'wu1', 'wd1', 'nmix', 'w_in', 'gm_ln_g', 'gm_ln_b', 'qn', 'kn')]
    in_specs = [rows(d_model)] + [_resident(a, l) for a in params] + [_resident(p['head_avg'], 0)]
    out_shape = [
        jax.ShapeDtypeStruct((n_tok, d_model), F32),
        jax.ShapeDtypeStruct((n_tok, d_conv), F32),
        jax.ShapeDtypeStruct((n_tok, d_gmlp), F32),
        jax.ShapeDtypeStruct((n_tok, d_gmlp), vg_dtype),
        jax.ShapeDtypeStruct((n_tok, d_att), BF16),
        jax.ShapeDtypeStruct((n_tok, d_att), BF16),
        jax.ShapeDtypeStruct((n_tok, d_att), BF16),
        jax.ShapeDtypeStruct((n_tail, d_att), F32),
        jax.ShapeDtypeStruct((n_tail, d_att), F32),
    ]
    out_specs = [rows(d_model), rows(d_conv), rows(d_gmlp), rows(d_gmlp),
                 rows(d_att), rows(d_att), rows(d_att), tail_spec, tail_spec]

    weight_bytes = (3 * _nbytes((d_model, d_ff), BF16) + _nbytes((d_model, d_in), BF16)
                    + _nbytes(p['head_avg'].shape[1:], BF16))
    block_bytes = 2 * (2 * _nbytes((tm, d_model), F32) + 2 * _nbytes((tm, d_conv), F32)
                       + _nbytes((tm, d_gmlp), F32) + 3 * _nbytes((tm, d_att), BF16)
                       + 2 * _nbytes((tail_rows, d_att), F32))
    scratch_bytes = _nbytes((tm, d_ff), BF16)
    temp_bytes = 6 * _nbytes((tm, d_model), F32)

    kernel = functools.partial(_token_kernel, tiles_per_seq=tiles_per_seq,
                               d_conv=d_conv, d_gmlp=d_gmlp, d_att=d_att)
    return pl.pallas_call(
        kernel,
        out_shape=out_shape,
        grid=(n_tok // tm,),
        in_specs=in_specs,
        out_specs=out_specs,
        scratch_shapes=[pltpu.VMEM((tm, d_ff), BF16)],
        compiler_params=pltpu.CompilerParams(
            dimension_semantics=("arbitrary",),
            vmem_limit_bytes=_vmem_limit(weight_bytes + block_bytes, scratch_bytes, temp_bytes)),
        name="token_kernel",
    )(x, *params, p['head_avg'])


def _mixer_kernel(x1_ref, gc_ref, gh_ref, u_ref, vg_ref, q_ref, kc_ref, kh_ref, vc_ref, vh_ref,
                  cw_ref, cb_ref, clg_ref, clb_ref, wcat_ref, gmb_ref, bias_ref,
                  wout_ref, n2_ref, wg_ref, wu_ref, wd_ref,
                  y_ref,
                  gbuf, mixed, a_scr, *, first_tile_has_no_past):
    nb_count, tt, d_model = x1_ref.shape
    d_conv = gc_ref.shape[2]
    d_gmlp = u_ref.shape[2]
    d_att = q_ref.shape[2]
    gm_rows = gmb_ref.shape[0]
    t = pl.program_id(1)
    slab = V7X_MXU_COLS
    n_slabs = d_att // slab

    head_of_lane = lax.broadcasted_iota(jnp.int32, (1, slab), 1) // HEAD_DIM
    gm_head_of_lane = lax.broadcasted_iota(jnp.int32, (1, d_gmlp), 1) // GM_HEAD_DIM
    key_index = lax.broadcasted_iota(jnp.int32, (1, BAND_LEN), 1)

    for nb in range(nb_count):
        r0 = nb * tt

        hist = gh_ref[nb]
        if first_tile_has_no_past:
            hist = jnp.where(t > 0, hist, 0.0)
        gbuf[0:CONV_HALO, :] = hist
        gbuf[CONV_HALO:, :] = gc_ref[nb]
        first_tap_row = CONV_HALO - CONV_PAD
        acc = None
        for r in range(V7X_SUBLANES):
            n_rows = tt if r == 0 else tt + V7X_SUBLANES
            part = None
            for off in range(r, CONV_HALO + 1, V7X_SUBLANES):
                w = off - first_tap_row
                if 0 <= w < CONV_WIDTH:
                    term = gbuf[off - r:off - r + n_rows, :] * cw_ref[w:w + 1, :]
                    part = term if part is None else part + term
            part = part[r:r + tt, :]
            acc = part if acc is None else acc + part
        yc = _layer_norm(acc + cb_ref[...], clg_ref[...], clb_ref[...])
        mixed[r0:r0 + tt, 0:d_conv] = (yc * jax.nn.sigmoid(yc)).astype(BF16)

        for ci in range(tt // gm_rows):
            rs = ci * gm_rows
            vgc = vg_ref[nb, rs:rs + gm_rows, :].astype(BF16)
            vstack = jnp.concatenate(
                [jnp.where(gm_head_of_lane == hh, vgc, jnp.zeros_like(vgc))
                 for hh in range(d_gmlp // GM_HEAD_DIM)], axis=0)
            s = _dot(wcat_ref[...], vstack) + gmb_ref[...]
            gm = u_ref[nb, rs:rs + gm_rows, :] * s
            mixed[r0 + rs:r0 + rs + gm_rows, d_conv:d_conv + d_gmlp] = gm.astype(BF16)

        for c in range(tt // CHUNK):
            cs = c * CHUNK
            for g in range(n_slabs):
                cols = slice(g * slab, (g + 1) * slab)
                q4 = q_ref[nb, cs:cs + CHUNK, cols]
                qs = jnp.concatenate(
                    [jnp.where(head_of_lane == hl, q4, jnp.zeros_like(q4))
                     for hl in range(HEADS_PER_SLAB)], axis=0)
                if cs < BAND_PAST:
                    kb = jnp.concatenate([kh_ref[nb, cs:, cols], kc_ref[nb, 0:cs + CHUNK, cols]], axis=0)
                    vb = jnp.concatenate([vh_ref[nb, cs:, cols], vc_ref[nb, 0:cs + CHUNK, cols]], axis=0)
                else:
                    kb = kc_ref[nb, cs - BAND_PAST:cs + CHUNK, cols]
                    vb = vc_ref[nb, cs - BAND_PAST:cs + CHUNK, cols]
                s = lax.dot_general(qs, kb, (((1,), (1,)), ((), ())), preferred_element_type=F32)
                bias = bias_ref[g]
                if first_tile_has_no_past and cs < BAND_PAST:
                    first_valid = jnp.where(t == 0, BAND_PAST - cs, 0)
                    bias = jnp.where(key_index >= first_valid, bias, MASK_VALUE)
                s = s + bias
                m = jnp.max(s, axis=-1, keepdims=True)
                e = jnp.exp(s - m)
                denom = jnp.sum(e, axis=-1, keepdims=True)
                o4 = _dot(e.astype(BF16), vb) * (1.0 / denom)
                o = o4[0:CHUNK, :]
                for hl in range(1, HEADS_PER_SLAB):
                    o = jnp.where(head_of_lane == hl, o4[hl * CHUNK:(hl + 1) * CHUNK, :], o)
                col = d_conv + d_gmlp + g * slab
                mixed[r0 + cs:r0 + cs + CHUNK, col:col + slab] = o.astype(BF16)

    x1 = x1_ref[...].reshape(nb_count * tt, d_model)
    d_cg = d_conv + d_gmlp
    x2 = x1 + _dot(mixed[:, d_cg:], wout_ref[d_cg:, :])
    x2 = x2 + _dot(mixed[:, 0:d_cg], wout_ref[0:d_cg, :])
    y = _swiglu_half_step(x2, n2_ref, wg_ref, wu_ref, wd_ref, a_scr)
    y_ref[...] = y.reshape(nb_count, tt, d_model)


def _mixer_call(x1, glu, glu_hist, u, vg, q, k, k_hist, v, v_hist, p, l, tables, *, nb, tt, self_history):
    n_seq, seq_len, d_model = x1.shape
    d_conv, d_gmlp, d_att = p['d_conv'], p['d_gmlp'], p['d_att']
    d_ff = p['wg2'].shape[2]
    assert n_seq % nb == 0 and seq_len % tt == 0 and tt % CHUNK == 0
    wcat, gmb, bias_tab = tables
    gm_rows = wcat.shape[1]
    assert tt % gm_rows == 0
    n_t = seq_len // tt
    rows = nb * tt

    def cur(width):
        return pl.BlockSpec((nb, tt, width), lambda b, t: (b, t, 0))

    if self_history:
        assert tt % BAND_PAST == 0 and glu_hist is None and k_hist is None and v_hist is None
        glu_hist, k_hist, v_hist = glu, k, v
        gh_spec = pl.BlockSpec((nb, CONV_HALO, d_conv),
                               lambda b, t: (b, jnp.maximum(t * (tt // CONV_HALO) - 1, 0), 0))
        past_spec = pl.BlockSpec((nb, BAND_PAST, d_att),
                                 lambda b, t: (b, jnp.maximum(t * (tt // BAND_PAST) - 1, 0), 0))
    else:
        assert n_t == 1
        gh_spec = pl.BlockSpec((None, nb, CONV_HALO, d_conv), lambda b, t: (l, b, 0, 0))
        past_spec = pl.BlockSpec((None, nb, BAND_PAST, d_att), lambda b, t: (l, b, 0, 0))

    params = ([p[name] for name in ('conv_w', 'conv_b', 'conv_ln_g', 'conv_ln_b')] + [wcat, gmb, bias_tab]
              + [p[name] for name in ('w_out', 'n2', 'wg2', 'wu2', 'wd2')])
    in_specs = [
        cur(d_model), cur(d_conv), gh_spec, cur(d_gmlp), cur(d_gmlp), cur(d_att),
        cur(d_att), past_spec, cur(d_att), past_spec,
    ] + [_resident(a, l) for a in params]
    scratch = [
        pltpu.VMEM((CONV_HALO + tt, d_conv), F32),
        pltpu.VMEM((rows, d_model), BF16),
        pltpu.VMEM((rows, d_ff), BF16),
    ]
    weight_bytes = (3 * _nbytes((d_model, d_ff), BF16) + _nbytes((d_model, d_model), BF16)
                    + _nbytes(bias_tab.shape[1:], F32)
                    + _nbytes((gm_rows, 4 * gm_rows), BF16) + _nbytes((gm_rows, d_gmlp), F32))
    block_bytes = 2 * (2 * _nbytes((rows, d_model), F32) + 2 * _nbytes((rows, d_conv), F32)
                       + _nbytes((rows, d_gmlp), F32) + 3 * _nbytes((rows, d_att), BF16)
                       + 2 * _nbytes((nb, BAND_PAST, d_att), BF16) + _nbytes((nb, CONV_HALO, d_conv), F32))
    scratch_bytes = (_nbytes((CONV_HALO + tt, d_conv), F32)
                     + _nbytes((rows, d_model), BF16) + _nbytes((rows, d_ff), BF16))
    temp_bytes = 6 * _nbytes((rows, d_model), F32)

    kernel = functools.partial(_mixer_kernel, first_tile_has_no_past=self_history)
    return pl.pallas_call(
        kernel,
        out_shape=jax.ShapeDtypeStruct((n_seq, seq_len, d_model), F32),
        grid=(n_seq // nb, n_t),
        in_specs=in_specs,
        out_specs=cur(d_model),
        scratch_shapes=scratch,
        compiler_params=pltpu.CompilerParams(
            dimension_semantics=("arbitrary", "arbitrary"),
            vmem_limit_bytes=_vmem_limit(weight_bytes + block_bytes, scratch_bytes, temp_bytes)),
        name="mixer_kernel",
    )(x1, glu, glu_hist, u, vg, q, k, k_hist, v, v_hist, *params)


def _rows(v):
    return v[:, None, :].astype(F32)


def _gmlp_tables(ws, bs, rows):
    depth, n_heads = ws.shape[:2]
    w = jnp.tril(ws[:, :, :rows, :rows])
    wcat = jnp.transpose(w, (0, 2, 1, 3)).reshape(depth, rows, n_heads * rows).astype(BF16)
    gmb = jnp.repeat(jnp.swapaxes(bs[:, :, :rows], 1, 2), GM_HEAD_DIM, axis=2).astype(F32)
    return wcat, gmb


def _band_bias(rel_bias):
    n_heads = rel_bias.shape[0]
    lo, hi = -(CHUNK - 1), BAND_LEN - 1
    assert -REL_CLIP <= lo and hi >= REL_CLIP
    by_rel = jnp.concatenate(
        [rel_bias[:, lo + REL_CLIP:], jnp.broadcast_to(rel_bias[:, -1:], (n_heads, hi - REL_CLIP))],
        axis=1).astype(F32)
    width = hi - lo + 1
    padded = jnp.pad(by_rel[:, ::-1], ((0, 0), (0, 1)))
    skew = jnp.tile(padded, (1, CHUNK))[:, :CHUNK * width].reshape(n_heads, CHUNK, width)
    tab = skew[:, :, CHUNK - 1:]
    return tab.reshape(n_heads // HEADS_PER_SLAB, HEADS_PER_SLAB * CHUNK, BAND_LEN)


def kernel(x_prompt, x_sample, cache_conv, cache_k, cache_v, ffn1_norm, ffn1_w_gate, ffn1_w_up, ffn1_w_down, mix_norm, w_in, conv_w, conv_b, conv_ln_g, conv_ln_b, gmlp_ln_g, gmlp_ln_b, gmlp_ws, gmlp_b, q_norm, k_norm, rel_bias, w_out, ffn2_norm, ffn2_w_gate, ffn2_w_up, ffn2_w_down):
    batch, seq, d_model = x_prompt.shape
    dec_batch, dec_seq, _ = x_sample.shape
    depth = w_in.shape[0]
    d_conv = conv_w.shape[2]
    d_gmlp = gmlp_ln_g.shape[1]
    n_att_heads = rel_bias.shape[1]
    d_att = n_att_heads * HEAD_DIM
    assert cache_k.shape[2] == BAND_PAST and cache_conv.shape[2] == CONV_PAD
    assert dec_seq == CHUNK and dec_seq <= GM_CHUNK

    head_avg = jnp.kron(jnp.eye(HEADS_PER_SLAB, dtype=F32),
                        jnp.full((HEAD_DIM, HEAD_DIM), 1.0 / HEAD_DIM, F32)).astype(BF16)[None]
    p = dict(
        d_conv=d_conv, d_gmlp=d_gmlp, d_att=d_att,
        n1=_rows(ffn1_norm), wg1=ffn1_w_gate.astype(BF16), wu1=ffn1_w_up.astype(BF16),
        wd1=ffn1_w_down.astype(BF16),
        nmix=_rows(mix_norm), w_in=w_in.astype(BF16),
        gm_ln_g=_rows(gmlp_ln_g), gm_ln_b=_rows(gmlp_ln_b),
        qn=_rows(jnp.tile(q_norm, (1, n_att_heads))), kn=_rows(jnp.tile(k_norm, (1, n_att_heads))),
        head_avg=head_avg,
        conv_w=conv_w.astype(F32), conv_b=_rows(conv_b),
        conv_ln_g=_rows(conv_ln_g), conv_ln_b=_rows(conv_ln_b),
        w_out=w_out.astype(BF16), n2=_rows(ffn2_norm), wg2=ffn2_w_gate.astype(BF16),
        wu2=ffn2_w_up.astype(BF16), wd2=ffn2_w_down.astype(BF16),
    )

    bias_tab = _band_bias(rel_bias.reshape(depth * n_att_heads, -1))
    bias_tab = bias_tab.reshape((depth, n_att_heads // HEADS_PER_SLAB) + bias_tab.shape[1:])
    prompt_tables = _gmlp_tables(gmlp_ws, gmlp_b, GM_CHUNK) + (bias_tab,)
    sample_tables = _gmlp_tables(gmlp_ws, gmlp_b, dec_seq) + (bias_tab,)
    conv_hist = jnp.pad(cache_conv.astype(F32), ((0, 0), (0, 0), (CONV_HALO - CONV_PAD, 0), (0, 0)))
    k_hist = cache_k.reshape(depth, dec_batch, BAND_PAST, d_att).astype(BF16)
    v_hist = cache_v.reshape(depth, dec_batch, BAND_PAST, d_att).astype(BF16)

    xp = x_prompt.reshape(batch * seq, d_model)
    xs = x_sample.reshape(dec_batch * dec_seq, d_model)
    outs = {name: [] for name in ('p_conv', 'p_k', 'p_v', 's_conv', 's_k', 's_v', 's_gv')}
    sample_streams_per_step = TOKEN_TILE // dec_seq // 2

    for l in range(depth):
        x1, glu, u, vg, q, k, v, k_tail, v_tail = _token_call(
            xp, p, l, seq_len=seq, tail_rows=min(BAND_PAST, seq), vg_dtype=BF16)
        seq3 = lambda a: a.reshape(batch, seq, a.shape[-1])
        xp = _mixer_call(seq3(x1), seq3(glu), None, seq3(u), seq3(vg), seq3(q), seq3(k), None, seq3(v), None,
                         p, l, prompt_tables, nb=1, tt=TOKEN_TILE, self_history=True).reshape(batch * seq, d_model)
        outs['p_conv'].append(seq3(glu)[:, seq - CONV_PAD:])
        outs['p_k'].append(k_tail.reshape(batch, -1, n_att_heads, HEAD_DIM))
        outs['p_v'].append(v_tail.reshape(batch, -1, n_att_heads, HEAD_DIM))

        x1, glu, u, vg, q, k, v, k_new, v_new = _token_call(
            xs, p, l, seq_len=dec_seq, tail_rows=TOKEN_TILE, vg_dtype=F32)
        dec3 = lambda a: a.reshape(dec_batch, dec_seq, a.shape[-1])
        xs = _mixer_call(dec3(x1), dec3(glu), conv_hist, dec3(u), dec3(vg), dec3(q), dec3(k), k_hist,
                         dec3(v), v_hist, p, l, sample_tables, nb=sample_streams_per_step, tt=dec_seq,
                         self_history=False).reshape(dec_batch * dec_seq, d_model)
        outs['s_conv'].append(dec3(glu)[:, dec_seq - CONV_PAD:])
        outs['s_k'].append(k_new.reshape(dec_batch, dec_seq, n_att_heads, HEAD_DIM))
        outs['s_v'].append(v_new.reshape(dec_batch, dec_seq, n_att_heads, HEAD_DIM))
        outs['s_gv'].append(vg.reshape(dec_batch, dec_seq, d_gmlp // GM_HEAD_DIM, GM_HEAD_DIM))

    return (xp.reshape(batch, seq, d_model), xs.reshape(dec_batch, dec_seq, d_model),
            jnp.stack(outs['p_conv']), jnp.stack(outs['p_k']), jnp.stack(outs['p_v']),
            jnp.stack(outs['s_conv']), jnp.stack(outs['s_k']), jnp.stack(outs['s_v']),
            jnp.stack(outs['s_gv']))
```

```python
import functools

import jax
import jax.numpy as jnp
from jax import lax
from jax.experimental import pallas as pl
from jax.experimental.pallas import tpu as pltpu

F32 = jnp.float32
BF16 = jnp.bfloat16

EPS = 1e-6
CHUNK = 64
BAND_CHUNKS = 8
BAND_PAST = BAND_CHUNKS * CHUNK
BAND_LEN = BAND_PAST + CHUNK
REL_CLIP = 128
CONV_WIDTH = 31
CONV_PAD = CONV_WIDTH - 1
CONV_HALO = 32
HEAD_DIM = 64
GM_HEAD_DIM = 64
GM_CHUNK = 128
MASK_VALUE = -1e30

V7X_LANES = 128
V7X_SUBLANES = 8
V7X_MXU_COLS = 256
V7X_VMEM_BYTES = 64 * 1024 * 1024
HEADS_PER_SLAB = V7X_MXU_COLS // HEAD_DIM

TOKEN_TILE = 512
FF_CHUNK = 256


def _vmem_limit(block_bytes, scratch_bytes, temp_bytes):
    return int(min(V7X_VMEM_BYTES - (4 << 20), block_bytes + scratch_bytes + temp_bytes))


def _nbytes(shape, dtype):
    n = 1
    for s in shape:
        n *= s
    return n * jnp.dtype(dtype).itemsize


def _rms_factors(x, g):
    return (x * g).astype(BF16), lax.rsqrt(jnp.mean(x * x, axis=-1, keepdims=True) + EPS)


def _layer_norm(x, g, b):
    mu = jnp.mean(x, axis=-1, keepdims=True)
    xc = x - mu
    var = jnp.mean(xc * xc, axis=-1, keepdims=True)
    return xc * lax.rsqrt(var + EPS) * g + b


def _dot(a, b):
    return jnp.dot(a, b, preferred_element_type=F32)


def _swiglu_half_step(x, norm_ref, wg_ref, wu_ref, wd_ref, a_scr):
    d_ff = wg_ref.shape[1]
    xg, r = _rms_factors(x, norm_ref[...])
    for c in range(0, d_ff, FF_CHUNK):
        w = min(FF_CHUNK, d_ff - c)
        g = r * _dot(xg, wg_ref[:, c:c + w])
        u = r * _dot(xg, wu_ref[:, c:c + w])
        a_scr[:, c:c + w] = (g * jax.nn.sigmoid(g) * u).astype(BF16)
    return x + 0.5 * _dot(a_scr[...], wd_ref[...])


def _token_kernel(x_ref, n1_ref, wg_ref, wu_ref, wd_ref, nmix_ref, win_ref, lng_ref, lnb_ref,
                  qn_ref, kn_ref, havg_ref,
                  x1_ref, glu_ref, u_ref, vg_ref, q_ref, k_ref, v_ref, ktail_ref, vtail_ref,
                  a_scr, *, tiles_per_seq, d_conv, d_gmlp, d_att):
    x1 = _swiglu_half_step(x_ref[...], n1_ref, wg_ref, wu_ref, wd_ref, a_scr)
    x1_ref[...] = x1
    h, r = _rms_factors(x1, nmix_ref[...])

    o1 = 2 * d_conv
    o2 = o1 + 2 * d_gmlp
    o3 = o2 + d_att
    o4 = o3 + d_att

    zc = r * _dot(h, win_ref[:, 0:o1])
    glu_ref[...] = zc[:, :d_conv] * jax.nn.sigmoid(zc[:, d_conv:])

    uv = jax.nn.gelu(r * _dot(h, win_ref[:, o1:o2]))
    u_ref[...] = uv[:, :d_gmlp]
    vg_ref[...] = _layer_norm(uv[:, d_gmlp:], lng_ref[...], lnb_ref[...]).astype(vg_ref.dtype)

    def head_rms(z, g):
        sq = (z * z).astype(BF16)
        slab = havg_ref.shape[0]
        ms = jnp.concatenate([_dot(sq[:, c:c + slab], havg_ref[...]) for c in range(0, z.shape[1], slab)], axis=1)
        return z * lax.rsqrt(ms + EPS) * g

    zq = r * _dot(h, win_ref[:, o2:o3])
    q_ref[...] = head_rms(zq, qn_ref[...] * (HEAD_DIM ** -0.5)).astype(BF16)
    zk = r * _dot(h, win_ref[:, o3:o4])
    k = head_rms(zk, kn_ref[...])
    k_ref[...] = k.astype(BF16)
    v = r * _dot(h, win_ref[:, o4:])
    v_ref[...] = v.astype(BF16)

    tail_rows = ktail_ref.shape[0]

    @pl.when(pl.program_id(0) % tiles_per_seq == tiles_per_seq - 1)
    def _():
        ktail_ref[...] = k[k.shape[0] - tail_rows:, :]
        vtail_ref[...] = v[v.shape[0] - tail_rows:, :]


def _resident(stack, l):
    shape = stack.shape[1:]
    return pl.BlockSpec((None,) + shape, lambda *_: (l,) + (0,) * len(shape), pipeline_mode=pl.Buffered(1))


def _token_call(x, p, l, *, seq_len, tail_rows, vg_dtype):
    n_tok, d_model = x.shape
    tm = TOKEN_TILE
    assert n_tok % tm == 0 and (seq_len % tm == 0 or tm % seq_len == 0)
    tiles_per_seq = max(seq_len // tm, 1)
    assert tail_rows <= tm
    n_tail = (n_tok // tm // tiles_per_seq) * tail_rows
    d_ff = p['wg1'].shape[2]
    d_in = p['w_in'].shape[2]
    d_conv, d_gmlp, d_att = p['d_conv'], p['d_gmlp'], p['d_att']

    def rows(width):
        return pl.BlockSpec((tm, width), lambda i: (i, 0))

    tail_spec = pl.BlockSpec((tail_rows, d_att), lambda i: (i // tiles_per_seq, 0))
    params = [p[name] for name in ('n1', 'wg1', 'wu1', 'wd1', 'nmix', 'w_in', 'gm_ln_g', 'gm_ln_b', 'qn', 'kn')]
    in_specs = [rows(d_model)] + [_resident(a, l) for a in params] + [_resident(p['head_avg'], 0)]
    out_shape = [
        jax.ShapeDtypeStruct((n_tok, d_model), F32),
        jax.ShapeDtypeStruct((n_tok, d_conv), F32),
        jax.ShapeDtypeStruct((n_tok, d_gmlp), F32),
        jax.ShapeDtypeStruct((n_tok, d_gmlp), vg_dtype),
        jax.ShapeDtypeStruct((n_tok, d_att), BF16),
        jax.ShapeDtypeStruct((n_tok, d_att), BF16),
        jax.ShapeDtypeStruct((n_tok, d_att), BF16),
        jax.ShapeDtypeStruct((n_tail, d_att), F32),
        jax.ShapeDtypeStruct((n_tail, d_att), F32),
    ]
    out_specs = [rows(d_model), rows(d_conv), rows(d_gmlp), rows(d_gmlp),
                 rows(d_att), rows(d_att), rows(d_att), tail_spec, tail_spec]

    weight_bytes = (3 * _nbytes((d_model, d_ff), BF16) + _nbytes((d_model, d_in), BF16)
                    + _nbytes(p['head_avg'].shape[1:], BF16))
    block_bytes = 2 * (2 * _nbytes((tm, d_model), F32) + 2 * _nbytes((tm, d_conv), F32)
                       + _nbytes((tm, d_gmlp), F32) + 3 * _nbytes((tm, d_att), BF16)
                       + 2 * _nbytes((tail_rows, d_att), F32))
    scratch_bytes = _nbytes((tm, d_ff), BF16)
    temp_bytes = 6 * _nbytes((tm, d_model), F32)

    kernel = functools.partial(_token_kernel, tiles_per_seq=tiles_per_seq,
                               d_conv=d_conv, d_gmlp=d_gmlp, d_att=d_att)
    return pl.pallas_call(
        kernel,
        out_shape=out_shape,
        grid=(n_tok // tm,),
        in_specs=in_specs,
        out_specs=out_specs,
        scratch_shapes=[pltpu.VMEM((tm, d_ff), BF16)],
        compiler_params=pltpu.CompilerParams(
            dimension_semantics=("arbitrary",),
            vmem_limit_bytes=_vmem_limit(weight_bytes + block_bytes, scratch_bytes, temp_bytes)),
        name="token_kernel",
    )(x, *params, p['head_avg'])


def _mixer_kernel(x1_ref, gc_ref, gh_ref, u_ref, vg_ref, q_ref, kc_ref, kh_ref, vc_ref, vh_ref,
                  cw_ref, cb_ref, clg_ref, clb_ref, wcat_ref, gmb_ref, bias_ref,
                  wout_ref, n2_ref, wg_ref, wu_ref, wd_ref,
                  y_ref,
                  gbuf, mixed, a_scr, *, first_tile_has_no_past):
    nb_count, tt, d_model = x1_ref.shape
    d_conv = gc_ref.shape[2]
    d_gmlp = u_ref.shape[2]
    d_att = q_ref.shape[2]
    gm_rows = gmb_ref.shape[0]
    t = pl.program_id(1)
    slab = V7X_MXU_COLS
    n_slabs = d_att // slab

    head_of_lane = lax.broadcasted_iota(jnp.int32, (1, slab), 1) // HEAD_DIM
    gm_head_of_lane = lax.broadcasted_iota(jnp.int32, (1, d_gmlp), 1) // GM_HEAD_DIM

    for nb in range(nb_count):
        r0 = nb * tt

        hist = gh_ref[nb]
        if first_tile_has_no_past:
            hist = jnp.where(t > 0, hist, 0.0)
        gbuf[0:CONV_HALO, :] = hist
        gbuf[CONV_HALO:, :] = gc_ref[nb]
        first_tap_row = CONV_HALO - CONV_PAD
        acc = None
        for r in range(V7X_SUBLANES):
            n_rows = tt if r == 0 else tt + V7X_SUBLANES
            part = None
            for off in range(r, CONV_HALO + 1, V7X_SUBLANES):
                w = off - first_tap_row
                if 0 <= w < CONV_WIDTH:
                    term = gbuf[off - r:off - r + n_rows, :] * cw_ref[w:w + 1, :]
                    part = term if part is None else part + term
            part = part[r:r + tt, :]
            acc = part if acc is None else acc + part
        yc = _layer_norm(acc + cb_ref[...], clg_ref[...], clb_ref[...])
        mixed[r0:r0 + tt, 0:d_conv] = (yc * jax.nn.sigmoid(yc)).astype(BF16)

        for ci in range(tt // gm_rows):
            rs = ci * gm_rows
            vgc = vg_ref[nb, rs:rs + gm_rows, :].astype(BF16)
            vstack = jnp.concatenate(
                [jnp.where(gm_head_of_lane == hh, vgc, jnp.zeros_like(vgc))
                 for hh in range(d_gmlp // GM_HEAD_DIM)], axis=0)
            s = _dot(wcat_ref[...], vstack) + gmb_ref[...]
            gm = u_ref[nb, rs:rs + gm_rows, :] * s
            mixed[r0 + rs:r0 + rs + gm_rows, d_conv:d_conv + d_gmlp] = gm.astype(BF16)

        for c in range(tt // CHUNK):
            cs = c * CHUNK
            for g in range(n_slabs):
                cols = slice(g * slab, (g + 1) * slab)
                q4 = q_ref[nb, cs:cs + CHUNK, cols]
                qs = jnp.concatenate(
                    [jnp.where(head_of_lane == hl, q4, jnp.zeros_like(q4))
                     for hl in range(HEADS_PER_SLAB)], axis=0)
                if cs < BAND_PAST:
                    kb = jnp.concatenate([kh_ref[nb, cs:, cols], kc_ref[nb, 0:cs + CHUNK, cols]], axis=0)
                    vb = jnp.concatenate([vh_ref[nb, cs:, cols], vc_ref[nb, 0:cs + CHUNK, cols]], axis=0)
                else:
                    kb = kc_ref[nb, cs - BAND_PAST:cs + CHUNK, cols]
                    vb = vc_ref[nb, cs - BAND_PAST:cs + CHUNK, cols]
                s = lax.dot_general(qs, kb, (((1,), (1,)), ((), ())), preferred_element_type=F32)
                if first_tile_has_no_past and cs < BAND_PAST:
                    bias = bias_ref[jnp.where(t == 0, 1 + c, 0), g]
                else:
                    bias = bias_ref[0, g]
                s = s + bias
                m = jnp.max(s, axis=-1, keepdims=True)
                e = jnp.exp(s - m)
                denom = jnp.sum(e, axis=-1, keepdims=True)
                o4 = _dot(e.astype(BF16), vb) * (1.0 / denom)
                o = o4[0:CHUNK, :]
                for hl in range(1, HEADS_PER_SLAB):
                    o = jnp.where(head_of_lane == hl, o4[hl * CHUNK:(hl + 1) * CHUNK, :], o)
                col = d_conv + d_gmlp + g * slab
                mixed[r0 + cs:r0 + cs + CHUNK, col:col + slab] = o.astype(BF16)

    x1 = x1_ref[...].reshape(nb_count * tt, d_model)
    d_cg = d_conv + d_gmlp
    x2 = x1 + _dot(mixed[:, d_cg:], wout_ref[d_cg:, :])
    x2 = x2 + _dot(mixed[:, 0:d_cg], wout_ref[0:d_cg, :])
    y = _swiglu_half_step(x2, n2_ref, wg_ref, wu_ref, wd_ref, a_scr)
    y_ref[...] = y.reshape(nb_count, tt, d_model)


def _mixer_call(x1, glu, glu_hist, u, vg, q, k, k_hist, v, v_hist, p, l, tables, *, nb, tt, self_history):
    n_seq, seq_len, d_model = x1.shape
    d_conv, d_gmlp, d_att = p['d_conv'], p['d_gmlp'], p['d_att']
    d_ff = p['wg2'].shape[2]
    assert n_seq % nb == 0 and seq_len % tt == 0 and tt % CHUNK == 0
    wcat, gmb, bias_tab = tables
    gm_rows = wcat.shape[1]
    assert tt % gm_rows == 0
    n_t = seq_len // tt
    rows = nb * tt

    def cur(width):
        return pl.BlockSpec((nb, tt, width), lambda b, t: (b, t, 0))

    if self_history:
        assert tt % BAND_PAST == 0 and glu_hist is None and k_hist is None and v_hist is None
        glu_hist, k_hist, v_hist = glu, k, v
        gh_spec = pl.BlockSpec((nb, CONV_HALO, d_conv),
                               lambda b, t: (b, jnp.maximum(t * (tt // CONV_HALO) - 1, 0), 0))
        past_spec = pl.BlockSpec((nb, BAND_PAST, d_att),
                                 lambda b, t: (b, jnp.maximum(t * (tt // BAND_PAST) - 1, 0), 0))
    else:
        assert n_t == 1
        gh_spec = pl.BlockSpec((None, nb, CONV_HALO, d_conv), lambda b, t: (l, b, 0, 0))
        past_spec = pl.BlockSpec((None, nb, BAND_PAST, d_att), lambda b, t: (l, b, 0, 0))

    params = ([p[name] for name in ('conv_w', 'conv_b', 'conv_ln_g', 'conv_ln_b')] + [wcat, gmb, bias_tab]
              + [p[name] for name in ('w_out', 'n2', 'wg2', 'wu2', 'wd2')])
    in_specs = [
        cur(d_model), cur(d_conv), gh_spec, cur(d_gmlp), cur(d_gmlp), cur(d_att),
        cur(d_att), past_spec, cur(d_att), past_spec,
    ] + [_resident(a, l) for a in params]
    scratch = [
        pltpu.VMEM((CONV_HALO + tt, d_conv), F32),
        pltpu.VMEM((rows, d_model), BF16),
        pltpu.VMEM((rows, d_ff), BF16),
    ]
    weight_bytes = (3 * _nbytes((d_model, d_ff), BF16) + _nbytes((d_model, d_model), BF16)
                    + _nbytes(bias_tab.shape[1:], F32)
                    + _nbytes((gm_rows, 4 * gm_rows), BF16) + _nbytes((gm_rows, d_gmlp), F32))
    block_bytes = 2 * (2 * _nbytes((rows, d_model), F32) + 2 * _nbytes((rows, d_conv), F32)
                       + _nbytes((rows, d_gmlp), F32) + 3 * _nbytes((rows, d_att), BF16)
                       + 2 * _nbytes((nb, BAND_PAST, d_att), BF16) + _nbytes((nb, CONV_HALO, d_conv), F32))
    scratch_bytes = (_nbytes((CONV_HALO + tt, d_conv), F32)
                     + _nbytes((rows, d_model), BF16) + _nbytes((rows, d_ff), BF16))
    temp_bytes = 6 * _nbytes((rows, d_model), F32)

    kernel = functools.partial(_mixer_kernel, first_tile_has_no_past=self_history)
    return pl.pallas_call(
        kernel,
        out_shape=jax.ShapeDtypeStruct((n_seq, seq_len, d_model), F32),
        grid=(n_seq // nb, n_t),
        in_specs=in_specs,
        out_specs=cur(d_model),
        scratch_shapes=scratch,
        compiler_params=pltpu.CompilerParams(
            dimension_semantics=("arbitrary", "arbitrary"),
            vmem_limit_bytes=_vmem_limit(weight_bytes + block_bytes, scratch_bytes, temp_bytes)),
        name="mixer_kernel",
    )(x1, glu, glu_hist, u, vg, q, k, k_hist, v, v_hist, *params)


def _rows(v):
    return v[:, None, :].astype(F32)


def _gmlp_tables(ws, bs, rows):
    depth, n_heads = ws.shape[:2]
    w = jnp.tril(ws[:, :, :rows, :rows])
    wcat = jnp.transpose(w, (0, 2, 1, 3)).reshape(depth, rows, n_heads * rows).astype(BF16)
    gmb = jnp.repeat(jnp.swapaxes(bs[:, :, :rows], 1, 2), GM_HEAD_DIM, axis=2).astype(F32)
    return wcat, gmb


def _band_bias(rel_bias):
    n_heads = rel_bias.shape[0]
    lo, hi = -(CHUNK - 1), BAND_LEN - 1
    assert -REL_CLIP <= lo and hi >= REL_CLIP
    by_rel = jnp.concatenate(
        [rel_bias[:, lo + REL_CLIP:], jnp.broadcast_to(rel_bias[:, -1:], (n_heads, hi - REL_CLIP))],
        axis=1).astype(F32)
    width = hi - lo + 1
    padded = jnp.pad(by_rel[:, ::-1], ((0, 0), (0, 1)))
    skew = jnp.tile(padded, (1, CHUNK))[:, :CHUNK * width].reshape(n_heads, CHUNK, width)
    tab = skew[:, :, CHUNK - 1:]
    return tab.reshape(n_heads // HEADS_PER_SLAB, HEADS_PER_SLAB * CHUNK, BAND_LEN)


def kernel(x_prompt, x_sample, cache_conv, cache_k, cache_v, ffn1_norm, ffn1_w_gate, ffn1_w_up, ffn1_w_down, mix_norm, w_in, conv_w, conv_b, conv_ln_g, conv_ln_b, gmlp_ln_g, gmlp_ln_b, gmlp_ws, gmlp_b, q_norm, k_norm, rel_bias, w_out, ffn2_norm, ffn2_w_gate, ffn2_w_up, ffn2_w_down):
    batch, seq, d_model = x_prompt.shape
    dec_batch, dec_seq, _ = x_sample.shape
    depth = w_in.shape[0]
    d_conv = conv_w.shape[2]
    d_gmlp = gmlp_ln_g.shape[1]
    n_att_heads = rel_bias.shape[1]
    d_att = n_att_heads * HEAD_DIM
    assert cache_k.shape[2] == BAND_PAST and cache_conv.shape[2] == CONV_PAD
    assert dec_seq == CHUNK and dec_seq <= GM_CHUNK

    head_avg = jnp.kron(jnp.eye(HEADS_PER_SLAB, dtype=F32),
                        jnp.full((HEAD_DIM, HEAD_DIM), 1.0 / HEAD_DIM, F32)).astype(BF16)[None]
    p = dict(
        d_conv=d_conv, d_gmlp=d_gmlp, d_att=d_att,
        n1=_rows(ffn1_norm), wg1=ffn1_w_gate.astype(BF16), wu1=ffn1_w_up.astype(BF16),
        wd1=ffn1_w_down.astype(BF16),
        nmix=_rows(mix_norm), w_in=w_in.astype(BF16),
        gm_ln_g=_rows(gmlp_ln_g), gm_ln_b=_rows(gmlp_ln_b),
        qn=_rows(jnp.tile(q_norm, (1, n_att_heads))), kn=_rows(jnp.tile(k_norm, (1, n_att_heads))),
        head_avg=head_avg,
        conv_w=conv_w.astype(F32), conv_b=_rows(conv_b),
        conv_ln_g=_rows(conv_ln_g), conv_ln_b=_rows(conv_ln_b),
        w_out=w_out.astype(BF16), n2=_rows(ffn2_norm), wg2=ffn2_w_gate.astype(BF16),
        wu2=ffn2_w_up.astype(BF16), wd2=ffn2_w_down.astype(BF16),
    )

    bias_tab = _band_bias(rel_bias.reshape(depth * n_att_heads, -1))
    bias_tab = bias_tab.reshape((depth, 1, n_att_heads // HEADS_PER_SLAB) + bias_tab.shape[1:])
    first_valid = BAND_PAST - CHUNK * jnp.arange(BAND_CHUNKS)
    before_start = jnp.arange(BAND_LEN)[None, :] < first_valid[:, None]
    masked = jnp.where(before_start[None, :, None, None, :], MASK_VALUE, bias_tab)
    prompt_tables = _gmlp_tables(gmlp_ws, gmlp_b, GM_CHUNK) + (jnp.concatenate([bias_tab, masked], axis=1),)
    sample_tables = _gmlp_tables(gmlp_ws, gmlp_b, dec_seq) + (bias_tab,)
    conv_hist = jnp.pad(cache_conv.astype(F32), ((0, 0), (0, 0), (CONV_HALO - CONV_PAD, 0), (0, 0)))
    k_hist = cache_k.reshape(depth, dec_batch, BAND_PAST, d_att).astype(BF16)
    v_hist = cache_v.reshape(depth, dec_batch, BAND_PAST, d_att).astype(BF16)

    xp = x_prompt.reshape(batch * seq, d_model)
    xs = x_sample.reshape(dec_batch * dec_seq, d_model)
    outs = {name: [] for name in ('p_conv', 'p_k', 'p_v', 's_conv', 's_k', 's_v', 's_gv')}
    sample_streams_per_step = TOKEN_TILE // dec_seq // 2

    for l in range(depth):
        x1, glu, u, vg, q, k, v, k_tail, v_tail = _token_call(
            xp, p, l, seq_len=seq, tail_rows=min(BAND_PAST, seq), vg_dtype=BF16)
        seq3 = lambda a: a.reshape(batch, seq, a.shape[-1])
        xp = _mixer_call(seq3(x1), seq3(glu), None, seq3(u), seq3(vg), seq3(q), seq3(k), None, seq3(v), None,
                         p, l, prompt_tables, nb=1, tt=TOKEN_TILE, self_history=True).reshape(batch * seq, d_model)
        outs['p_conv'].append(seq3(glu)[:, seq - CONV_PAD:])
        outs['p_k'].append(k_tail.reshape(batch, -1, n_att_heads, HEAD_DIM))
        outs['p_v'].append(v_tail.reshape(batch, -1, n_att_heads, HEAD_DIM))

        x1, glu, u, vg, q, k, v, k_new, v_new = _token_call(
            xs, p, l, seq_len=dec_seq, tail_rows=TOKEN_TILE, vg_dtype=F32)
        dec3 = lambda a: a.reshape(dec_batch, dec_seq, a.shape[-1])
        xs = _mixer_call(dec3(x1), dec3(glu), conv_hist, dec3(u), dec3(vg), dec3(q), dec3(k), k_hist,
                         dec3(v), v_hist, p, l, sample_tables, nb=sample_streams_per_step, tt=dec_seq,
                         self_history=False).reshape(dec_batch * dec_seq, d_model)
        outs['s_conv'].append(dec3(glu)[:, dec_seq - CONV_PAD:])
        outs['s_k'].append(k_new.reshape(dec_batch, dec_seq, n_att_heads, HEAD_DIM))
        outs['s_v'].append(v_new.reshape(dec_batch, dec_seq, n_att_heads, HEAD_DIM))
        outs['s_gv'].append(vg.reshape(dec_batch, dec_seq, d_gmlp // GM_HEAD_DIM, GM_HEAD_DIM))

    return (xp.reshape(batch, seq, d_model), xs.reshape(dec_batch, dec_seq, d_model),
            jnp.stack(outs['p_conv']), jnp.stack(outs['p_k']), jnp.stack(outs['p_v']),
            jnp.stack(outs['s_conv']), jnp.stack(outs['s_k']), jnp.stack(outs['s_v']),
            jnp.stack(outs['s_gv']))
```

```python
import functools

import jax
import jax.numpy as jnp
from jax import lax
from jax.experimental import pallas as pl
from jax.experimental.pallas import tpu as pltpu

F32 = jnp.float32
BF16 = jnp.bfloat16

EPS = 1e-6
CHUNK = 64
BAND_CHUNKS = 8
BAND_PAST = BAND_CHUNKS * CHUNK
BAND_LEN = BAND_PAST + CHUNK
REL_CLIP = 128
CONV_WIDTH = 31
CONV_PAD = CONV_WIDTH - 1
CONV_HALO = 32
HEAD_DIM = 64
GM_HEAD_DIM = 64
GM_CHUNK = 128
MASK_VALUE = -1e30

V7X_LANES = 128
V7X_SUBLANES = 8
V7X_MXU_COLS = 256
V7X_VMEM_BYTES = 64 * 1024 * 1024
V7X_VMEM_RESERVED_BYTES = 4 * 1024 * 1024
HEADS_PER_SLAB = V7X_MXU_COLS // HEAD_DIM

TOKEN_TILE = 512
FF_CHUNK = 256


def _vmem_limit(block_bytes, scratch_bytes, temp_bytes):
    return int(min(V7X_VMEM_BYTES - V7X_VMEM_RESERVED_BYTES, block_bytes + scratch_bytes + temp_bytes))


def _nbytes(shape, dtype):
    n = 1
    for s in shape:
        n *= s
    return n * jnp.dtype(dtype).itemsize


def _rms_factors(x, g):
    return (x * g).astype(BF16), lax.rsqrt(jnp.mean(x * x, axis=-1, keepdims=True) + EPS)


def _layer_norm(x, g, b):
    mu = jnp.mean(x, axis=-1, keepdims=True)
    xc = x - mu
    var = jnp.mean(xc * xc, axis=-1, keepdims=True)
    return xc * lax.rsqrt(var + EPS) * g + b


def _dot(a, b):
    return jnp.dot(a, b, preferred_element_type=F32)


def _swiglu_half_step(x, norm_ref, wg_ref, wu_ref, wd_ref, a_scr):
    d_ff = wg_ref.shape[1]
    xg, r = _rms_factors(x, norm_ref[...])
    for c in range(0, d_ff, FF_CHUNK):
        w = min(FF_CHUNK, d_ff - c)
        g = r * _dot(xg, wg_ref[:, c:c + w])
        u = r * _dot(xg, wu_ref[:, c:c + w])
        a_scr[:, c:c + w] = (g * jax.nn.sigmoid(g) * u).astype(BF16)
    return x + 0.5 * _dot(a_scr[...], wd_ref[...])


def _token_kernel(x_ref, n1_ref, wg_ref, wu_ref, wd_ref, nmix_ref, win_ref, lng_ref, lnb_ref,
                  qn_ref, kn_ref, havg_ref,
                  x1_ref, glu_ref, u_ref, vg_ref, q_ref, k_ref, v_ref, ktail_ref, vtail_ref,
                  a_scr, *, tiles_per_seq, d_conv, d_gmlp, d_att):
    x1 = _swiglu_half_step(x_ref[...], n1_ref, wg_ref, wu_ref, wd_ref, a_scr)
    x1_ref[...] = x1
    h, r = _rms_factors(x1, nmix_ref[...])

    o1 = 2 * d_conv
    o2 = o1 + 2 * d_gmlp
    o3 = o2 + d_att
    o4 = o3 + d_att

    zc = r * _dot(h, win_ref[:, 0:o1])
    glu_ref[...] = zc[:, :d_conv] * jax.nn.sigmoid(zc[:, d_conv:])

    uv = jax.nn.gelu(r * _dot(h, win_ref[:, o1:o2]))
    u_ref[...] = uv[:, :d_gmlp]
    vg_ref[...] = _layer_norm(uv[:, d_gmlp:], lng_ref[...], lnb_ref[...]).astype(vg_ref.dtype)

    def head_rms(z, g):
        sq = (z * z).astype(BF16)
        slab = havg_ref.shape[0]
        ms = jnp.concatenate([_dot(sq[:, c:c + slab], havg_ref[...]) for c in range(0, z.shape[1], slab)], axis=1)
        return z * lax.rsqrt(ms + EPS) * g

    zq = r * _dot(h, win_ref[:, o2:o3])
    q_ref[...] = head_rms(zq, qn_ref[...] * (HEAD_DIM ** -0.5)).astype(BF16)
    zk = r * _dot(h, win_ref[:, o3:o4])
    k = head_rms(zk, kn_ref[...])
    k_ref[...] = k.astype(BF16)
    v = r * _dot(h, win_ref[:, o4:])
    v_ref[...] = v.astype(BF16)

    tail_rows = ktail_ref.shape[0]

    @pl.when(pl.program_id(0) % tiles_per_seq == tiles_per_seq - 1)
    def _():
        ktail_ref[...] = k[k.shape[0] - tail_rows:, :]
        vtail_ref[...] = v[v.shape[0] - tail_rows:, :]


def _resident(stack, l):
    shape = stack.shape[1:]
    return pl.BlockSpec((None,) + shape, lambda *_: (l,) + (0,) * len(shape), pipeline_mode=pl.Buffered(1))


def _token_call(x, p, l, *, seq_len, tail_rows, vg_dtype):
    n_tok, d_model = x.shape
    tm = TOKEN_TILE
    assert n_tok % tm == 0 and (seq_len % tm == 0 or tm % seq_len == 0)
    tiles_per_seq = max(seq_len // tm, 1)
    assert tail_rows <= tm
    n_tail = (n_tok // tm // tiles_per_seq) * tail_rows
    d_ff = p['wg1'].shape[2]
    d_in = p['w_in'].shape[2]
    d_conv, d_gmlp, d_att = p['d_conv'], p['d_gmlp'], p['d_att']

    def rows(width):
        return pl.BlockSpec((tm, width), lambda i: (i, 0))

    tail_spec = pl.BlockSpec((tail_rows, d_att), lambda i: (i // tiles_per_seq, 0))
    params = [p[name] for name in ('n1', 'wg1', 'wu1', 'wd1', 'nmix', 'w_in', 'gm_ln_g', 'gm_ln_b', 'qn', 'kn')]
    in_specs = [rows(d_model)] + [_resident(a, l) for a in params] + [_resident(p['head_avg'], 0)]
    out_shape = [
        jax.ShapeDtypeStruct((n_tok, d_model), F32),
        jax.ShapeDtypeStruct((n_tok, d_conv), F32),
        jax.ShapeDtypeStruct((n_tok, d_gmlp), F32),
        jax.ShapeDtypeStruct((n_tok, d_gmlp), vg_dtype),
        jax.ShapeDtypeStruct((n_tok, d_att), BF16),
        jax.ShapeDtypeStruct((n_tok, d_att), BF16),
        jax.ShapeDtypeStruct((n_tok, d_att), BF16),
        jax.ShapeDtypeStruct((n_tail, d_att), F32),
        jax.ShapeDtypeStruct((n_tail, d_att), F32),
    ]
    out_specs = [rows(d_model), rows(d_conv), rows(d_gmlp), rows(d_gmlp),
                 rows(d_att), rows(d_att), rows(d_att), tail_spec, tail_spec]

    weight_bytes = (3 * _nbytes((d_model, d_ff), BF16) + _nbytes((d_model, d_in), BF16)
                    + _nbytes(p['head_avg'].shape[1:], BF16))
    block_bytes = 2 * (2 * _nbytes((tm, d_model), F32) + 2 * _nbytes((tm, d_conv), F32)
                       + _nbytes((tm, d_gmlp), F32) + 3 * _nbytes((tm, d_att), BF16)
                       + 2 * _nbytes((tail_rows, d_att), F32))
    scratch_bytes = _nbytes((tm, d_ff), BF16)
    temp_bytes = 6 * _nbytes((tm, d_model), F32)

    kernel = functools.partial(_token_kernel, tiles_per_seq=tiles_per_seq,
                               d_conv=d_conv, d_gmlp=d_gmlp, d_att=d_att)
    return pl.pallas_call(
        kernel,
        out_shape=out_shape,
        grid=(n_tok // tm,),
        in_specs=in_specs,
        out_specs=out_specs,
        scratch_shapes=[pltpu.VMEM((tm, d_ff), BF16)],
        compiler_params=pltpu.CompilerParams(
            dimension_semantics=("arbitrary",),
            vmem_limit_bytes=_vmem_limit(weight_bytes + block_bytes, scratch_bytes, temp_bytes)),
        name="token_kernel",
    )(x, *params, p['head_avg'])


def _mixer_kernel(x1_ref, gc_ref, gh_ref, u_ref, vg_ref, q_ref, kc_ref, kh_ref, vc_ref, vh_ref,
                  cw_ref, cb_ref, clg_ref, clb_ref, wcat_ref, gmb_ref, bias_ref,
                  wout_ref, n2_ref, wg_ref, wu_ref, wd_ref,
                  y_ref,
                  gbuf, mixed, a_scr, *, first_tile_has_no_past):
    nb_count, tt, d_model = x1_ref.shape
    d_conv = gc_ref.shape[2]
    d_gmlp = u_ref.shape[2]
    d_att = q_ref.shape[2]
    gm_rows = gmb_ref.shape[0]
    t = pl.program_id(1)
    slab = V7X_MXU_COLS
    n_slabs = d_att // slab

    head_of_lane = lax.broadcasted_iota(jnp.int32, (1, slab), 1) // HEAD_DIM
    gm_head_of_lane = lax.broadcasted_iota(jnp.int32, (1, d_gmlp), 1) // GM_HEAD_DIM
    key_index = lax.broadcasted_iota(jnp.int32, (1, BAND_LEN), 1)

    for nb in range(nb_count):
        r0 = nb * tt

        hist = gh_ref[nb]
        if first_tile_has_no_past:
            hist = jnp.where(t > 0, hist, 0.0)
        gbuf[0:CONV_HALO, :] = hist
        gbuf[CONV_HALO:, :] = gc_ref[nb]
        first_tap_row = CONV_HALO - CONV_PAD
        acc = None
        for r in range(V7X_SUBLANES):
            n_rows = tt if r == 0 else tt + V7X_SUBLANES
            part = None
            for off in range(r, CONV_HALO + 1, V7X_SUBLANES):
                w = off - first_tap_row
                if 0 <= w < CONV_WIDTH:
                    term = gbuf[off - r:off - r + n_rows, :] * cw_ref[w:w + 1, :]
                    part = term if part is None else part + term
            part = part[r:r + tt, :]
            acc = part if acc is None else acc + part
        yc = _layer_norm(acc + cb_ref[...], clg_ref[...], clb_ref[...])
        mixed[r0:r0 + tt, 0:d_conv] = (yc * jax.nn.sigmoid(yc)).astype(BF16)

        for ci in range(tt // gm_rows):
            rs = ci * gm_rows
            vgc = vg_ref[nb, rs:rs + gm_rows, :].astype(BF16)
            vstack = jnp.concatenate(
                [jnp.where(gm_head_of_lane == hh, vgc, jnp.zeros_like(vgc))
                 for hh in range(d_gmlp // GM_HEAD_DIM)], axis=0)
            s = _dot(wcat_ref[...], vstack) + gmb_ref[...]
            gm = u_ref[nb, rs:rs + gm_rows, :] * s
            mixed[r0 + rs:r0 + rs + gm_rows, d_conv:d_conv + d_gmlp] = gm.astype(BF16)

        for c in range(tt // CHUNK):
            cs = c * CHUNK
            for g in range(n_slabs):
                cols = slice(g * slab, (g + 1) * slab)
                q4 = q_ref[nb, cs:cs + CHUNK, cols]
                qs = jnp.concatenate(
                    [jnp.where(head_of_lane == hl, q4, jnp.zeros_like(q4))
                     for hl in range(HEADS_PER_SLAB)], axis=0)
                if cs < BAND_PAST:
                    kb = jnp.concatenate([kh_ref[nb, cs:, cols], kc_ref[nb, 0:cs + CHUNK, cols]], axis=0)
                    vb = jnp.concatenate([vh_ref[nb, cs:, cols], vc_ref[nb, 0:cs + CHUNK, cols]], axis=0)
                else:
                    kb = kc_ref[nb, cs - BAND_PAST:cs + CHUNK, cols]
                    vb = vc_ref[nb, cs - BAND_PAST:cs + CHUNK, cols]
                s = lax.dot_general(qs, kb, (((1,), (1,)), ((), ())), preferred_element_type=F32)
                bias = bias_ref[g]
                if first_tile_has_no_past and cs < BAND_PAST:
                    first_valid = jnp.where(t == 0, BAND_PAST - cs, 0)
                    bias = jnp.where(key_index >= first_valid, bias, MASK_VALUE)
                s = s + bias
                m = jnp.max(s, axis=-1, keepdims=True)
                e = jnp.exp(s - m)
                denom = jnp.sum(e, axis=-1, keepdims=True)
                o4 = _dot(e.astype(BF16), vb) * (1.0 / denom)
                o = o4[0:CHUNK, :]
                for hl in range(1, HEADS_PER_SLAB):
                    o = jnp.where(head_of_lane == hl, o4[hl * CHUNK:(hl + 1) * CHUNK, :], o)
                col = d_conv + d_gmlp + g * slab
                mixed[r0 + cs:r0 + cs + CHUNK, col:col + slab] = o.astype(BF16)

    x1 = x1_ref[...].reshape(nb_count * tt, d_model)
    d_cg = d_conv + d_gmlp
    x2 = x1 + _dot(mixed[:, d_cg:], wout_ref[d_cg:, :])
    x2 = x2 + _dot(mixed[:, 0:d_cg], wout_ref[0:d_cg, :])
    y = _swiglu_half_step(x2, n2_ref, wg_ref, wu_ref, wd_ref, a_scr)
    y_ref[...] = y.reshape(nb_count, tt, d_model)


def _mixer_call(x1, glu, glu_hist, u, vg, q, k, k_hist, v, v_hist, p, l, tables, *, nb, tt, self_history):
    n_seq, seq_len, d_model = x1.shape
    d_conv, d_gmlp, d_att = p['d_conv'], p['d_gmlp'], p['d_att']
    d_ff = p['wg2'].shape[2]
    assert n_seq % nb == 0 and seq_len % tt == 0 and tt % CHUNK == 0
    wcat, gmb, bias_tab = tables
    gm_rows = wcat.shape[1]
    assert tt % gm_rows == 0
    n_t = seq_len // tt
    rows = nb * tt

    def cur(width):
        return pl.BlockSpec((nb, tt, width), lambda b, t: (b, t, 0))

    if self_history:
        assert tt % BAND_PAST == 0 and glu_hist is None and k_hist is None and v_hist is None
        glu_hist, k_hist, v_hist = glu, k, v
        gh_spec = pl.BlockSpec((nb, CONV_HALO, d_conv),
                               lambda b, t: (b, jnp.maximum(t * (tt // CONV_HALO) - 1, 0), 0))
        past_spec = pl.BlockSpec((nb, BAND_PAST, d_att),
                                 lambda b, t: (b, jnp.maximum(t * (tt // BAND_PAST) - 1, 0), 0))
    else:
        assert n_t == 1
        gh_spec = pl.BlockSpec((None, nb, CONV_HALO, d_conv), lambda b, t: (l, b, 0, 0))
        past_spec = pl.BlockSpec((None, nb, BAND_PAST, d_att), lambda b, t: (l, b, 0, 0))

    params = ([p[name] for name in ('conv_w', 'conv_b', 'conv_ln_g', 'conv_ln_b')] + [wcat, gmb, bias_tab]
              + [p[name] for name in ('w_out', 'n2', 'wg2', 'wu2', 'wd2')])
    in_specs = [
        cur(d_model), cur(d_conv), gh_spec, cur(d_gmlp), cur(d_gmlp), cur(d_att),
        cur(d_att), past_spec, cur(d_att), past_spec,
    ] + [_resident(a, l) for a in params]
    scratch = [
        pltpu.VMEM((CONV_HALO + tt, d_conv), F32),
        pltpu.VMEM((rows, d_model), BF16),
        pltpu.VMEM((rows, d_ff), BF16),
    ]
    weight_bytes = (3 * _nbytes((d_model, d_ff), BF16) + _nbytes((d_model, d_model), BF16)
                    + _nbytes(bias_tab.shape[1:], F32)
                    + _nbytes((gm_rows, 4 * gm_rows), BF16) + _nbytes((gm_rows, d_gmlp), F32))
    block_bytes = 2 * (2 * _nbytes((rows, d_model), F32) + 2 * _nbytes((rows, d_conv), F32)
                       + _nbytes((rows, d_gmlp), F32) + 3 * _nbytes((rows, d_att), BF16)
                       + 2 * _nbytes((nb, BAND_PAST, d_att), BF16) + _nbytes((nb, CONV_HALO, d_conv), F32))
    scratch_bytes = (_nbytes((CONV_HALO + tt, d_conv), F32)
                     + _nbytes((rows, d_model), BF16) + _nbytes((rows, d_ff), BF16))
    temp_bytes = 6 * _nbytes((rows, d_model), F32)

    kernel = functools.partial(_mixer_kernel, first_tile_has_no_past=self_history)
    return pl.pallas_call(
        kernel,
        out_shape=jax.ShapeDtypeStruct((n_seq, seq_len, d_model), F32),
        grid=(n_seq // nb, n_t),
        in_specs=in_specs,
        out_specs=cur(d_model),
        scratch_shapes=scratch,
        compiler_params=pltpu.CompilerParams(
            dimension_semantics=("arbitrary", "arbitrary"),
            vmem_limit_bytes=_vmem_limit(weight_bytes + block_bytes, scratch_bytes, temp_bytes)),
        name="mixer_kernel",
    )(x1, glu, glu_hist, u, vg, q, k, k_hist, v, v_hist, *params)


def _rows(v):
    return v[:, None, :].astype(F32)


def _gmlp_tables(ws, bs, rows):
    depth, n_heads = ws.shape[:2]
    w = jnp.tril(ws[:, :, :rows, :rows])
    wcat = jnp.transpose(w, (0, 2, 1, 3)).reshape(depth, rows, n_heads * rows).astype(BF16)
    gmb = jnp.repeat(jnp.swapaxes(bs[:, :, :rows], 1, 2), GM_HEAD_DIM, axis=2).astype(F32)
    return wcat, gmb


def _band_bias(rel_bias):
    n_heads = rel_bias.shape[0]
    lo, hi = -(CHUNK - 1), BAND_LEN - 1
    assert -REL_CLIP <= lo and hi >= REL_CLIP
    by_rel = jnp.concatenate(
        [rel_bias[:, lo + REL_CLIP:], jnp.broadcast_to(rel_bias[:, -1:], (n_heads, hi - REL_CLIP))],
        axis=1).astype(F32)
    width = hi - lo + 1
    padded = jnp.pad(by_rel[:, ::-1], ((0, 0), (0, 1)))
    skew = jnp.tile(padded, (1, CHUNK))[:, :CHUNK * width].reshape(n_heads, CHUNK, width)
    tab = skew[:, :, CHUNK - 1:]
    return tab.reshape(n_heads // HEADS_PER_SLAB, HEADS_PER_SLAB * CHUNK, BAND_LEN)


def kernel(x_prompt, x_sample, cache_conv, cache_k, cache_v, ffn1_norm, ffn1_w_gate, ffn1_w_up, ffn1_w_down, mix_norm, w_in, conv_w, conv_b, conv_ln_g, conv_ln_b, gmlp_ln_g, gmlp_ln_b, gmlp_ws, gmlp_b, q_norm, k_norm, rel_bias, w_out, ffn2_norm, ffn2_w_gate, ffn2_w_up, ffn2_w_down):
    batch, seq, d_model = x_prompt.shape
    dec_batch, dec_seq, _ = x_sample.shape
    depth = w_in.shape[0]
    d_conv = conv_w.shape[2]
    d_gmlp = gmlp_ln_g.shape[1]
    n_att_heads = rel_bias.shape[1]
    d_att = n_att_heads * HEAD_DIM
    assert cache_k.shape[2] == BAND_PAST and cache_conv.shape[2] == CONV_PAD
    assert dec_seq == CHUNK and dec_seq <= GM_CHUNK

    head_avg = jnp.kron(jnp.eye(HEADS_PER_SLAB, dtype=F32),
                        jnp.full((HEAD_DIM, HEAD_DIM), 1.0 / HEAD_DIM, F32)).astype(BF16)[None]
    p = dict(
        d_conv=d_conv, d_gmlp=d_gmlp, d_att=d_att,
        n1=_rows(ffn1_norm), wg1=ffn1_w_gate.astype(BF16), wu1=ffn1_w_up.astype(BF16),
        wd1=ffn1_w_down.astype(BF16),
        nmix=_rows(mix_norm), w_in=w_in.astype(BF16),
        gm_ln_g=_rows(gmlp_ln_g), gm_ln_b=_rows(gmlp_ln_b),
        qn=_rows(jnp.tile(q_norm, (1, n_att_heads))), kn=_rows(jnp.tile(k_norm, (1, n_att_heads))),
        head_avg=head_avg,
        conv_w=conv_w.astype(F32), conv_b=_rows(conv_b),
        conv_ln_g=_rows(conv_ln_g), conv_ln_b=_rows(conv_ln_b),
        w_out=w_out.astype(BF16), n2=_rows(ffn2_norm), wg2=ffn2_w_gate.astype(BF16),
        wu2=ffn2_w_up.astype(BF16), wd2=ffn2_w_down.astype(BF16),
    )

    bias_tab = _band_bias(rel_bias.reshape(depth * n_att_heads, -1))
    bias_tab = bias_tab.reshape((depth, n_att_heads // HEADS_PER_SLAB) + bias_tab.shape[1:])
    prompt_tables = _gmlp_tables(gmlp_ws, gmlp_b, GM_CHUNK) + (bias_tab,)
    sample_tables = _gmlp_tables(gmlp_ws, gmlp_b, dec_seq) + (bias_tab,)
    conv_hist = jnp.pad(cache_conv.astype(F32), ((0, 0), (0, 0), (CONV_HALO - CONV_PAD, 0), (0, 0)))
    k_hist = cache_k.reshape(depth, dec_batch, BAND_PAST, d_att).astype(BF16)
    v_hist = cache_v.reshape(depth, dec_batch, BAND_PAST, d_att).astype(BF16)

    xp = x_prompt.reshape(batch * seq, d_model)
    xs = x_sample.reshape(dec_batch * dec_seq, d_model)
    outs = {name: [] for name in ('p_conv', 'p_k', 'p_v', 's_conv', 's_k', 's_v', 's_gv')}
    sample_streams_per_step = TOKEN_TILE // dec_seq // 2

    for l in range(depth):
        x1, glu, u, vg, q, k, v, k_tail, v_tail = _token_call(
            xp, p, l, seq_len=seq, tail_rows=min(BAND_PAST, seq), vg_dtype=BF16)
        seq3 = lambda a: a.reshape(batch, seq, a.shape[-1])
        xp = _mixer_call(seq3(x1), seq3(glu), None, seq3(u), seq3(vg), seq3(q), seq3(k), None, seq3(v), None,
                         p, l, prompt_tables, nb=1, tt=TOKEN_TILE, self_history=True).reshape(batch * seq, d_model)
        outs['p_conv'].append(seq3(glu)[:, seq - CONV_PAD:])
        outs['p_k'].append(k_tail.reshape(batch, -1, n_att_heads, HEAD_DIM))
        outs['p_v'].append(v_tail.reshape(batch, -1, n_att_heads, HEAD_DIM))

        x1, glu, u, vg, q, k, v, k_new, v_new = _token_call(
            xs, p, l, seq_len=dec_seq, tail_rows=TOKEN_TILE, vg_dtype=F32)
        dec3 = lambda a: a.reshape(dec_batch, dec_seq, a.shape[-1])
        xs = _mixer_call(dec3(x1), dec3(glu), conv_hist, dec3(u), dec3(vg), dec3(q), dec3(k), k_hist,
                         dec3(v), v_hist, p, l, sample_tables, nb=sample_streams_per_step, tt=dec_seq,
                         self_history=False).reshape(dec_batch * dec_seq, d_model)
        outs['s_conv'].append(dec3(glu)[:, dec_seq - CONV_PAD:])
        outs['s_k'].append(k_new.reshape(dec_batch, dec_seq, n_att_heads, HEAD_DIM))
        outs['s_v'].append(v_new.reshape(dec_batch, dec_seq, n_att_heads, HEAD_DIM))
        outs['s_gv'].append(vg.reshape(dec_batch, dec_seq, d_gmlp // GM_HEAD_DIM, GM_HEAD_DIM))

    return (xp.reshape(batch, seq, d_model), xs.reshape(dec_batch, dec_seq, d_model),
            jnp.stack(outs['p_conv']), jnp.stack(outs['p_k']), jnp.stack(outs['p_v']),
            jnp.stack(outs['s_conv']), jnp.stack(outs['s_k']), jnp.stack(outs['s_v']),
            jnp.stack(outs['s_gv']))
```

```python
import functools

import jax
import jax.numpy as jnp
from jax import lax
from jax.experimental import pallas as pl
from jax.experimental.pallas import tpu as pltpu

F32 = jnp.float32
BF16 = jnp.bfloat16

EPS = 1e-6
CHUNK = 64
BAND_CHUNKS = 8
BAND_PAST = BAND_CHUNKS * CHUNK
BAND_LEN = BAND_PAST + CHUNK
REL_CLIP = 128
CONV_WIDTH = 31
CONV_PAD = CONV_WIDTH - 1
CONV_HALO = 32
HEAD_DIM = 64
GM_HEAD_DIM = 64
GM_CHUNK = 128
MASK_VALUE = -1e30

V7X_LANES = 128
V7X_SUBLANES = 8
V7X_MXU_COLS = 256
V7X_VMEM_BYTES = 64 * 1024 * 1024
V7X_VMEM_RESERVED_BYTES = 4 * 1024 * 1024
HEADS_PER_SLAB = V7X_MXU_COLS // HEAD_DIM

TOKEN_TILE = 512
FF_CHUNK = 256


def _vmem_limit(block_bytes, scratch_bytes, temp_bytes):
    return int(min(V7X_VMEM_BYTES - V7X_VMEM_RESERVED_BYTES, block_bytes + scratch_bytes + temp_bytes))


def _nbytes(shape, dtype):
    n = 1
    for s in shape:
        n *= s
    return n * jnp.dtype(dtype).itemsize


def _rms_factors(x, g):
    return (x * g).astype(BF16), lax.rsqrt(jnp.mean(x * x, axis=-1, keepdims=True) + EPS)


def _layer_norm(x, g, b):
    mu = jnp.mean(x, axis=-1, keepdims=True)
    xc = x - mu
    var = jnp.mean(xc * xc, axis=-1, keepdims=True)
    return xc * lax.rsqrt(var + EPS) * g + b


def _dot(a, b):
    return jnp.dot(a, b, preferred_element_type=F32)


def _swiglu_half_step(x, norm_ref, wg_ref, wu_ref, wd_ref, a_scr):
    d_ff = wg_ref.shape[1]
    xg, r = _rms_factors(x, norm_ref[...])
    for c in range(0, d_ff, FF_CHUNK):
        w = min(FF_CHUNK, d_ff - c)
        g = r * _dot(xg, wg_ref[:, c:c + w])
        u = r * _dot(xg, wu_ref[:, c:c + w])
        a_scr[:, c:c + w] = (g * jax.nn.sigmoid(g) * u).astype(BF16)
    return x + 0.5 * _dot(a_scr[...], wd_ref[...])


def _token_kernel(x_ref, n1_ref, wg_ref, wu_ref, wd_ref, nmix_ref, win_ref, lng_ref, lnb_ref,
                  qn_ref, kn_ref, havg_ref,
                  x1_ref, gu_ref, kvqg_ref, ktail_ref, vtail_ref, *rest,
                  tiles_per_seq, d_conv, d_gmlp, d_att):
    *vg32_ref, a_scr = rest
    x1 = _swiglu_half_step(x_ref[...], n1_ref, wg_ref, wu_ref, wd_ref, a_scr)
    x1_ref[...] = x1
    h, r = _rms_factors(x1, nmix_ref[...])

    o1 = 2 * d_conv
    o2 = o1 + 2 * d_gmlp
    o3 = o2 + d_att
    o4 = o3 + d_att

    zc = r * _dot(h, win_ref[:, 0:o1])
    gu_ref[:, 0:d_conv] = zc[:, :d_conv] * jax.nn.sigmoid(zc[:, d_conv:])

    uv = jax.nn.gelu(r * _dot(h, win_ref[:, o1:o2]))
    gu_ref[:, d_conv:] = uv[:, :d_gmlp]
    vg = _layer_norm(uv[:, d_gmlp:], lng_ref[...], lnb_ref[...])
    kvqg_ref[:, 3 * d_att:] = vg.astype(BF16)
    if vg32_ref:
        vg32_ref[0][...] = vg

    def head_rms(z, g):
        sq = (z * z).astype(BF16)
        slab = havg_ref.shape[0]
        ms = jnp.concatenate([_dot(sq[:, c:c + slab], havg_ref[...]) for c in range(0, z.shape[1], slab)], axis=1)
        return z * lax.rsqrt(ms + EPS) * g

    zq = r * _dot(h, win_ref[:, o2:o3])
    kvqg_ref[:, 2 * d_att:3 * d_att] = head_rms(zq, qn_ref[...] * (HEAD_DIM ** -0.5)).astype(BF16)
    zk = r * _dot(h, win_ref[:, o3:o4])
    k = head_rms(zk, kn_ref[...])
    kvqg_ref[:, 0:d_att] = k.astype(BF16)
    v = r * _dot(h, win_ref[:, o4:])
    kvqg_ref[:, d_att:2 * d_att] = v.astype(BF16)

    tail_rows = ktail_ref.shape[0]

    @pl.when(pl.program_id(0) % tiles_per_seq == tiles_per_seq - 1)
    def _():
        ktail_ref[...] = k[k.shape[0] - tail_rows:, :]
        vtail_ref[...] = v[v.shape[0] - tail_rows:, :]


def _resident(stack, l):
    shape = stack.shape[1:]
    return pl.BlockSpec((None,) + shape, lambda *_: (l,) + (0,) * len(shape), pipeline_mode=pl.Buffered(1))


def _token_call(x, p, l, *, seq_len, tail_rows, with_vg_f32):
    n_tok, d_model = x.shape
    tm = TOKEN_TILE
    assert n_tok % tm == 0 and (seq_len % tm == 0 or tm % seq_len == 0)
    tiles_per_seq = max(seq_len // tm, 1)
    assert tail_rows <= tm
    n_tail = (n_tok // tm // tiles_per_seq) * tail_rows
    d_ff = p['wg1'].shape[2]
    d_in = p['w_in'].shape[2]
    d_conv, d_gmlp, d_att = p['d_conv'], p['d_gmlp'], p['d_att']

    def rows(width):
        return pl.BlockSpec((tm, width), lambda i: (i, 0))

    tail_spec = pl.BlockSpec((tail_rows, d_att), lambda i: (i // tiles_per_seq, 0))
    params = [p[name] for name in ('n1', 'wg1', 'wu1', 'wd1', 'nmix', 'w_in', 'gm_ln_g', 'gm_ln_b', 'qn', 'kn')]
    in_specs = [rows(d_model)] + [_resident(a, l) for a in params] + [_resident(p['head_avg'], 0)]
    d_gu, d_kvqg = d_conv + d_gmlp, 3 * d_att + d_gmlp
    out_shape = [
        jax.ShapeDtypeStruct((n_tok, d_model), F32),
        jax.ShapeDtypeStruct((n_tok, d_gu), F32),
        jax.ShapeDtypeStruct((n_tok, d_kvqg), BF16),
        jax.ShapeDtypeStruct((n_tail, d_att), F32),
        jax.ShapeDtypeStruct((n_tail, d_att), F32),
    ]
    out_specs = [rows(d_model), rows(d_gu), rows(d_kvqg), tail_spec, tail_spec]
    if with_vg_f32:
        out_shape.append(jax.ShapeDtypeStruct((n_tok, d_gmlp), F32))
        out_specs.append(rows(d_gmlp))

    weight_bytes = (3 * _nbytes((d_model, d_ff), BF16) + _nbytes((d_model, d_in), BF16)
                    + _nbytes(p['head_avg'].shape[1:], BF16))
    block_bytes = 2 * (2 * _nbytes((tm, d_model), F32) + _nbytes((tm, d_gu), F32)
                       + _nbytes((tm, d_gmlp), F32) + _nbytes((tm, d_kvqg), BF16)
                       + 2 * _nbytes((tail_rows, d_att), F32))
    scratch_bytes = _nbytes((tm, d_ff), BF16)
    temp_bytes = 6 * _nbytes((tm, d_model), F32)

    kernel = functools.partial(_token_kernel, tiles_per_seq=tiles_per_seq,
                               d_conv=d_conv, d_gmlp=d_gmlp, d_att=d_att)
    return pl.pallas_call(
        kernel,
        out_shape=out_shape,
        grid=(n_tok // tm,),
        in_specs=in_specs,
        out_specs=out_specs,
        scratch_shapes=[pltpu.VMEM((tm, d_ff), BF16)],
        compiler_params=pltpu.CompilerParams(
            dimension_semantics=("arbitrary",),
            vmem_limit_bytes=_vmem_limit(weight_bytes + block_bytes, scratch_bytes, temp_bytes)),
        name="token_kernel",
    )(x, *params, p['head_avg'])


def _mixer_kernel(x1_ref, gu_ref, gh_ref, kvqg_ref, kvh_ref,
                  cw_ref, cb_ref, clg_ref, clb_ref, wcat_ref, gmb_ref, bias_ref,
                  wout_ref, n2_ref, wg_ref, wu_ref, wd_ref,
                  y_ref,
                  gbuf, mixed, a_scr, *, first_tile_has_no_past, d_conv, d_gmlp, d_att):
    nb_count, tt, d_model = x1_ref.shape
    k_col, v_col, q_col, vg_col = 0, d_att, 2 * d_att, 3 * d_att
    gm_rows = gmb_ref.shape[0]
    t = pl.program_id(1)
    slab = V7X_MXU_COLS
    n_slabs = d_att // slab

    head_of_lane = lax.broadcasted_iota(jnp.int32, (1, slab), 1) // HEAD_DIM
    gm_head_of_lane = lax.broadcasted_iota(jnp.int32, (1, d_gmlp), 1) // GM_HEAD_DIM
    key_index = lax.broadcasted_iota(jnp.int32, (1, BAND_LEN), 1)

    for nb in range(nb_count):
        r0 = nb * tt

        hist = gh_ref[nb, :, 0:d_conv]
        if first_tile_has_no_past:
            hist = jnp.where(t > 0, hist, 0.0)
        gbuf[0:CONV_HALO, :] = hist
        gbuf[CONV_HALO:, :] = gu_ref[nb, :, 0:d_conv]
        first_tap_row = CONV_HALO - CONV_PAD
        acc = None
        for r in range(V7X_SUBLANES):
            n_rows = tt if r == 0 else tt + V7X_SUBLANES
            part = None
            for off in range(r, CONV_HALO + 1, V7X_SUBLANES):
                w = off - first_tap_row
                if 0 <= w < CONV_WIDTH:
                    term = gbuf[off - r:off - r + n_rows, :] * cw_ref[w:w + 1, :]
                    part = term if part is None else part + term
            part = part[r:r + tt, :]
            acc = part if acc is None else acc + part
        yc = _layer_norm(acc + cb_ref[...], clg_ref[...], clb_ref[...])
        mixed[r0:r0 + tt, 0:d_conv] = (yc * jax.nn.sigmoid(yc)).astype(BF16)

        for ci in range(tt // gm_rows):
            rs = ci * gm_rows
            vgc = kvqg_ref[nb, rs:rs + gm_rows, vg_col:]
            vstack = jnp.concatenate(
                [jnp.where(gm_head_of_lane == hh, vgc, jnp.zeros_like(vgc))
                 for hh in range(d_gmlp // GM_HEAD_DIM)], axis=0)
            s = _dot(wcat_ref[...], vstack) + gmb_ref[...]
            gm = gu_ref[nb, rs:rs + gm_rows, d_conv:] * s
            mixed[r0 + rs:r0 + rs + gm_rows, d_conv:d_conv + d_gmlp] = gm.astype(BF16)

        for c in range(tt // CHUNK):
            cs = c * CHUNK
            for g in range(n_slabs):
                q4 = kvqg_ref[nb, cs:cs + CHUNK, q_col + g * slab:q_col + (g + 1) * slab]
                kcols = slice(k_col + g * slab, k_col + (g + 1) * slab)
                vcols = slice(v_col + g * slab, v_col + (g + 1) * slab)
                qs = jnp.concatenate(
                    [jnp.where(head_of_lane == hl, q4, jnp.zeros_like(q4))
                     for hl in range(HEADS_PER_SLAB)], axis=0)
                if cs < BAND_PAST:
                    kb = jnp.concatenate([kvh_ref[nb, cs:, kcols], kvqg_ref[nb, 0:cs + CHUNK, kcols]], axis=0)
                    vb = jnp.concatenate([kvh_ref[nb, cs:, vcols], kvqg_ref[nb, 0:cs + CHUNK, vcols]], axis=0)
                else:
                    kb = kvqg_ref[nb, cs - BAND_PAST:cs + CHUNK, kcols]
                    vb = kvqg_ref[nb, cs - BAND_PAST:cs + CHUNK, vcols]
                s = lax.dot_general(qs, kb, (((1,), (1,)), ((), ())), preferred_element_type=F32)
                bias = bias_ref[g]
                if first_tile_has_no_past and cs < BAND_PAST:
                    first_valid = jnp.where(t == 0, BAND_PAST - cs, 0)
                    bias = jnp.where(key_index >= first_valid, bias, MASK_VALUE)
                s = s + bias
                m = jnp.max(s, axis=-1, keepdims=True)
                e = jnp.exp(s - m)
                denom = jnp.sum(e, axis=-1, keepdims=True)
                o4 = _dot(e.astype(BF16), vb) * (1.0 / denom)
                o = o4[0:CHUNK, :]
                for hl in range(1, HEADS_PER_SLAB):
                    o = jnp.where(head_of_lane == hl, o4[hl * CHUNK:(hl + 1) * CHUNK, :], o)
                col = d_conv + d_gmlp + g * slab
                mixed[r0 + cs:r0 + cs + CHUNK, col:col + slab] = o.astype(BF16)

    x1 = x1_ref[...].reshape(nb_count * tt, d_model)
    d_cg = d_conv + d_gmlp
    x2 = x1 + _dot(mixed[:, d_cg:], wout_ref[d_cg:, :])
    x2 = x2 + _dot(mixed[:, 0:d_cg], wout_ref[0:d_cg, :])
    y = _swiglu_half_step(x2, n2_ref, wg_ref, wu_ref, wd_ref, a_scr)
    y_ref[...] = y.reshape(nb_count, tt, d_model)


def _mixer_call(x1, gu, glu_hist, kvqg, kv_hist, p, l, tables, *, nb, tt, self_history):
    n_seq, seq_len, d_model = x1.shape
    d_conv, d_gmlp, d_att = p['d_conv'], p['d_gmlp'], p['d_att']
    d_ff = p['wg2'].shape[2]
    assert n_seq % nb == 0 and seq_len % tt == 0 and tt % CHUNK == 0
    wcat, gmb, bias_tab = tables
    gm_rows = wcat.shape[1]
    assert tt % gm_rows == 0
    n_t = seq_len // tt
    rows = nb * tt
    d_gu, d_kvqg = d_conv + d_gmlp, 3 * d_att + d_gmlp
    assert gu.shape[2] == d_gu and kvqg.shape[2] == d_kvqg

    def cur(width):
        return pl.BlockSpec((nb, tt, width), lambda b, t: (b, t, 0))

    if self_history:
        assert tt % BAND_PAST == 0 and glu_hist is None and kv_hist is None
        glu_hist, kv_hist = gu, kvqg
        gh_spec = pl.BlockSpec((nb, CONV_HALO, d_gu),
                               lambda b, t: (b, jnp.maximum(t * (tt // CONV_HALO) - 1, 0), 0))
        past_spec = pl.BlockSpec((nb, BAND_PAST, 2 * d_att),
                                 lambda b, t: (b, jnp.maximum(t * (tt // BAND_PAST) - 1, 0), 0))
    else:
        assert n_t == 1
        gh_spec = pl.BlockSpec((None, nb, CONV_HALO, d_conv), lambda b, t: (l, b, 0, 0))
        past_spec = pl.BlockSpec((None, nb, BAND_PAST, 2 * d_att), lambda b, t: (l, b, 0, 0))

    params = ([p[name] for name in ('conv_w', 'conv_b', 'conv_ln_g', 'conv_ln_b')] + [wcat, gmb, bias_tab]
              + [p[name] for name in ('w_out', 'n2', 'wg2', 'wu2', 'wd2')])
    in_specs = [cur(d_model), cur(d_gu), gh_spec, cur(d_kvqg), past_spec] + [_resident(a, l) for a in params]
    scratch = [
        pltpu.VMEM((CONV_HALO + tt, d_conv), F32),
        pltpu.VMEM((rows, d_model), BF16),
        pltpu.VMEM((rows, d_ff), BF16),
    ]
    weight_bytes = (3 * _nbytes((d_model, d_ff), BF16) + _nbytes((d_model, d_model), BF16)
                    + _nbytes(bias_tab.shape[1:], F32)
                    + _nbytes((gm_rows, 4 * gm_rows), BF16) + _nbytes((gm_rows, d_gmlp), F32))
    block_bytes = 2 * (2 * _nbytes((rows, d_model), F32) + 2 * _nbytes((rows, d_conv), F32)
                       + _nbytes((rows, d_gmlp), F32) + 3 * _nbytes((rows, d_att), BF16)
                       + 2 * _nbytes((nb, BAND_PAST, d_att), BF16) + _nbytes((nb, CONV_HALO, d_conv), F32))
    scratch_bytes = (_nbytes((CONV_HALO + tt, d_conv), F32)
                     + _nbytes((rows, d_model), BF16) + _nbytes((rows, d_ff), BF16))
    temp_bytes = 6 * _nbytes((rows, d_model), F32)

    kernel = functools.partial(_mixer_kernel, first_tile_has_no_past=self_history,
                               d_conv=d_conv, d_gmlp=d_gmlp, d_att=d_att)
    return pl.pallas_call(
        kernel,
        out_shape=jax.ShapeDtypeStruct((n_seq, seq_len, d_model), F32),
        grid=(n_seq // nb, n_t),
        in_specs=in_specs,
        out_specs=cur(d_model),
        scratch_shapes=scratch,
        compiler_params=pltpu.CompilerParams(
            dimension_semantics=("arbitrary", "arbitrary"),
            vmem_limit_bytes=_vmem_limit(weight_bytes + block_bytes, scratch_bytes, temp_bytes)),
        name="mixer_kernel",
    )(x1, gu, glu_hist, kvqg, kv_hist, *params)


def _rows(v):
    return v[:, None, :].astype(F32)


def _gmlp_tables(ws, bs, rows):
    depth, n_heads = ws.shape[:2]
    w = jnp.tril(ws[:, :, :rows, :rows])
    wcat = jnp.transpose(w, (0, 2, 1, 3)).reshape(depth, rows, n_heads * rows).astype(BF16)
    gmb = jnp.repeat(jnp.swapaxes(bs[:, :, :rows], 1, 2), GM_HEAD_DIM, axis=2).astype(F32)
    return wcat, gmb


def _band_bias(rel_bias):
    n_heads = rel_bias.shape[0]
    lo, hi = -(CHUNK - 1), BAND_LEN - 1
    assert -REL_CLIP <= lo and hi >= REL_CLIP
    by_rel = jnp.concatenate(
        [rel_bias[:, lo + REL_CLIP:], jnp.broadcast_to(rel_bias[:, -1:], (n_heads, hi - REL_CLIP))],
        axis=1).astype(F32)
    width = hi - lo + 1
    padded = jnp.pad(by_rel[:, ::-1], ((0, 0), (0, 1)))
    skew = jnp.tile(padded, (1, CHUNK))[:, :CHUNK * width].reshape(n_heads, CHUNK, width)
    tab = skew[:, :, CHUNK - 1:]
    return tab.reshape(n_heads // HEADS_PER_SLAB, HEADS_PER_SLAB * CHUNK, BAND_LEN)


def kernel(x_prompt, x_sample, cache_conv, cache_k, cache_v, ffn1_norm, ffn1_w_gate, ffn1_w_up, ffn1_w_down, mix_norm, w_in, conv_w, conv_b, conv_ln_g, conv_ln_b, gmlp_ln_g, gmlp_ln_b, gmlp_ws, gmlp_b, q_norm, k_norm, rel_bias, w_out, ffn2_norm, ffn2_w_gate, ffn2_w_up, ffn2_w_down):
    batch, seq, d_model = x_prompt.shape
    dec_batch, dec_seq, _ = x_sample.shape
    depth = w_in.shape[0]
    d_conv = conv_w.shape[2]
    d_gmlp = gmlp_ln_g.shape[1]
    n_att_heads = rel_bias.shape[1]
    d_att = n_att_heads * HEAD_DIM
    assert cache_k.shape[2] == BAND_PAST and cache_conv.shape[2] == CONV_PAD
    assert dec_seq == CHUNK and dec_seq <= GM_CHUNK

    head_avg = jnp.kron(jnp.eye(HEADS_PER_SLAB, dtype=F32),
                        jnp.full((HEAD_DIM, HEAD_DIM), 1.0 / HEAD_DIM, F32)).astype(BF16)[None]
    p = dict(
        d_conv=d_conv, d_gmlp=d_gmlp, d_att=d_att,
        n1=_rows(ffn1_norm), wg1=ffn1_w_gate.astype(BF16), wu1=ffn1_w_up.astype(BF16),
        wd1=ffn1_w_down.astype(BF16),
        nmix=_rows(mix_norm), w_in=w_in.astype(BF16),
        gm_ln_g=_rows(gmlp_ln_g), gm_ln_b=_rows(gmlp_ln_b),
        qn=_rows(jnp.tile(q_norm, (1, n_att_heads))), kn=_rows(jnp.tile(k_norm, (1, n_att_heads))),
        head_avg=head_avg,
        conv_w=conv_w.astype(F32), conv_b=_rows(conv_b),
        conv_ln_g=_rows(conv_ln_g), conv_ln_b=_rows(conv_ln_b),
        w_out=w_out.astype(BF16), n2=_rows(ffn2_norm), wg2=ffn2_w_gate.astype(BF16),
        wu2=ffn2_w_up.astype(BF16), wd2=ffn2_w_down.astype(BF16),
    )

    bias_tab = _band_bias(rel_bias.reshape(depth * n_att_heads, -1))
    bias_tab = bias_tab.reshape((depth, n_att_heads // HEADS_PER_SLAB) + bias_tab.shape[1:])
    prompt_tables = _gmlp_tables(gmlp_ws, gmlp_b, GM_CHUNK) + (bias_tab,)
    sample_tables = _gmlp_tables(gmlp_ws, gmlp_b, dec_seq) + (bias_tab,)
    conv_hist = jnp.pad(cache_conv.astype(F32), ((0, 0), (0, 0), (CONV_HALO - CONV_PAD, 0), (0, 0)))
    kv_hist = jnp.concatenate([cache_k.reshape(depth, dec_batch, BAND_PAST, d_att),
                               cache_v.reshape(depth, dec_batch, BAND_PAST, d_att)], axis=-1).astype(BF16)

    xp = x_prompt.reshape(batch * seq, d_model)
    xs = x_sample.reshape(dec_batch * dec_seq, d_model)
    outs = {name: [] for name in ('p_conv', 'p_k', 'p_v', 's_conv', 's_k', 's_v', 's_gv')}
    sample_streams_per_step = TOKEN_TILE // dec_seq // 2

    for l in range(depth):
        x1, gu, kvqg, k_tail, v_tail = _token_call(
            xp, p, l, seq_len=seq, tail_rows=min(BAND_PAST, seq), with_vg_f32=False)
        seq3 = lambda a: a.reshape(batch, seq, a.shape[-1])
        xp = _mixer_call(seq3(x1), seq3(gu), None, seq3(kvqg), None, p, l, prompt_tables,
                         nb=1, tt=TOKEN_TILE, self_history=True).reshape(batch * seq, d_model)
        outs['p_conv'].append(seq3(gu)[:, seq - CONV_PAD:, :d_conv])
        outs['p_k'].append(k_tail.reshape(batch, -1, n_att_heads, HEAD_DIM))
        outs['p_v'].append(v_tail.reshape(batch, -1, n_att_heads, HEAD_DIM))

        x1, gu, kvqg, k_new, v_new, vg = _token_call(
            xs, p, l, seq_len=dec_seq, tail_rows=TOKEN_TILE, with_vg_f32=True)
        dec3 = lambda a: a.reshape(dec_batch, dec_seq, a.shape[-1])
        xs = _mixer_call(dec3(x1), dec3(gu), conv_hist, dec3(kvqg), kv_hist, p, l, sample_tables,
                         nb=sample_streams_per_step, tt=dec_seq,
                         self_history=False).reshape(dec_batch * dec_seq, d_model)
        outs['s_conv'].append(dec3(gu)[:, dec_seq - CONV_PAD:, :d_conv])
        outs['s_k'].append(k_new.reshape(dec_batch, dec_seq, n_att_heads, HEAD_DIM))
        outs['s_v'].append(v_new.reshape(dec_batch, dec_seq, n_att_heads, HEAD_DIM))
        outs['s_gv'].append(vg.reshape(dec_batch, dec_seq, d_gmlp // GM_HEAD_DIM, GM_HEAD_DIM))

    return (xp.reshape(batch, seq, d_model), xs.reshape(dec_batch, dec_seq, d_model),
            jnp.stack(outs['p_conv']), jnp.stack(outs['p_k']), jnp.stack(outs['p_v']),
            jnp.stack(outs['s_conv']), jnp.stack(outs['s_k']), jnp.stack(outs['s_v']),
            jnp.stack(outs['s_gv']))
```

```python
import functools

import jax
import jax.numpy as jnp
from jax import lax
from jax.experimental import pallas as pl
from jax.experimental.pallas import tpu as pltpu

F32 = jnp.float32
BF16 = jnp.bfloat16

EPS = 1e-6
CHUNK = 64
BAND_CHUNKS = 8
BAND_PAST = BAND_CHUNKS * CHUNK
BAND_LEN = BAND_PAST + CHUNK
REL_CLIP = 128
CONV_WIDTH = 31
CONV_PAD = CONV_WIDTH - 1
CONV_HALO = 32
HEAD_DIM = 64
GM_HEAD_DIM = 64
GM_CHUNK = 128
MASK_VALUE = -1e30

V7X_LANES = 128
V7X_SUBLANES = 8
V7X_MXU_COLS = 256
V7X_VMEM_BYTES = 64 * 1024 * 1024
V7X_VMEM_RESERVED_BYTES = 4 * 1024 * 1024
HEADS_PER_SLAB = V7X_MXU_COLS // HEAD_DIM

TOKEN_TILE = 512
FF_CHUNK = 256


def _vmem_limit(block_bytes, scratch_bytes, temp_bytes):
    return int(min(V7X_VMEM_BYTES - V7X_VMEM_RESERVED_BYTES, block_bytes + scratch_bytes + temp_bytes))


def _nbytes(shape, dtype):
    n = 1
    for s in shape:
        n *= s
    return n * jnp.dtype(dtype).itemsize


def _rms_factors(x, g):
    return (x * g).astype(BF16), lax.rsqrt(jnp.mean(x * x, axis=-1, keepdims=True) + EPS)


def _layer_norm(x, g, b):
    mu = jnp.mean(x, axis=-1, keepdims=True)
    xc = x - mu
    var = jnp.mean(xc * xc, axis=-1, keepdims=True)
    return xc * lax.rsqrt(var + EPS) * g + b


def _dot(a, b):
    return jnp.dot(a, b, preferred_element_type=F32)


def _swiglu_half_step(x, norm_ref, wg_ref, wu_ref, wd_ref, a_scr):
    d_ff = wg_ref.shape[1]
    xg, r = _rms_factors(x, norm_ref[...])
    for c in range(0, d_ff, FF_CHUNK):
        w = min(FF_CHUNK, d_ff - c)
        g = r * _dot(xg, wg_ref[:, c:c + w])
        u = r * _dot(xg, wu_ref[:, c:c + w])
        a_scr[:, c:c + w] = (g * jax.nn.sigmoid(g) * u).astype(BF16)
    return x + 0.5 * _dot(a_scr[...], wd_ref[...])


def _token_kernel(x_ref, n1_ref, wg_ref, wu_ref, wd_ref, nmix_ref, win_ref, lng_ref, lnb_ref,
                  qn_ref, kn_ref, havg_ref,
                  x1_ref, gu_ref, kvqg_ref, ktail_ref, vtail_ref, *rest,
                  tiles_per_seq, d_conv, d_gmlp, d_att):
    *vg32_ref, a_scr = rest
    x1 = _swiglu_half_step(x_ref[...], n1_ref, wg_ref, wu_ref, wd_ref, a_scr)
    x1_ref[...] = x1
    h, r = _rms_factors(x1, nmix_ref[...])

    o1 = 2 * d_conv
    o2 = o1 + 2 * d_gmlp
    o3 = o2 + d_att
    o4 = o3 + d_att

    zc = r * _dot(h, win_ref[:, 0:o1])
    gu_ref[:, 0:d_conv] = zc[:, :d_conv] * jax.nn.sigmoid(zc[:, d_conv:])

    uv = jax.nn.gelu(r * _dot(h, win_ref[:, o1:o2]))
    gu_ref[:, d_conv:] = uv[:, :d_gmlp]
    vg = _layer_norm(uv[:, d_gmlp:], lng_ref[...], lnb_ref[...])
    kvqg_ref[:, 3 * d_att:] = vg.astype(BF16)
    if vg32_ref:
        vg32_ref[0][...] = vg

    def head_rms(z, g):
        sq = (z * z).astype(BF16)
        slab = havg_ref.shape[0]
        ms = jnp.concatenate([_dot(sq[:, c:c + slab], havg_ref[...]) for c in range(0, z.shape[1], slab)], axis=1)
        return z * lax.rsqrt(ms + EPS) * g

    zq = r * _dot(h, win_ref[:, o2:o3])
    kvqg_ref[:, 2 * d_att:3 * d_att] = head_rms(zq, qn_ref[...] * (HEAD_DIM ** -0.5)).astype(BF16)
    zk = r * _dot(h, win_ref[:, o3:o4])
    k = head_rms(zk, kn_ref[...])
    kvqg_ref[:, 0:d_att] = k.astype(BF16)
    v = r * _dot(h, win_ref[:, o4:])
    kvqg_ref[:, d_att:2 * d_att] = v.astype(BF16)

    tail_rows = ktail_ref.shape[0]

    @pl.when(pl.program_id(0) % tiles_per_seq == tiles_per_seq - 1)
    def _():
        ktail_ref[...] = k[k.shape[0] - tail_rows:, :]
        vtail_ref[...] = v[v.shape[0] - tail_rows:, :]


def _resident(stack, l):
    shape = stack.shape[1:]
    return pl.BlockSpec((None,) + shape, lambda *_: (l,) + (0,) * len(shape), pipeline_mode=pl.Buffered(1))


def _token_call(x, p, l, *, seq_len, tail_rows, with_vg_f32):
    n_tok, d_model = x.shape
    tm = TOKEN_TILE
    assert n_tok % tm == 0 and (seq_len % tm == 0 or tm % seq_len == 0)
    tiles_per_seq = max(seq_len // tm, 1)
    assert tail_rows <= tm
    n_tail = (n_tok // tm // tiles_per_seq) * tail_rows
    d_ff = p['wg1'].shape[2]
    d_in = p['w_in'].shape[2]
    d_conv, d_gmlp, d_att = p['d_conv'], p['d_gmlp'], p['d_att']

    def rows(width):
        return pl.BlockSpec((tm, width), lambda i: (i, 0))

    tail_spec = pl.BlockSpec((tail_rows, d_att), lambda i: (i // tiles_per_seq, 0))
    params = [p[name] for name in ('n1', 'wg1', 'wu1', 'wd1', 'nmix', 'w_in', 'gm_ln_g', 'gm_ln_b', 'qn', 'kn')]
    in_specs = [rows(d_model)] + [_resident(a, l) for a in params] + [_resident(p['head_avg'], 0)]
    d_gu, d_kvqg = d_conv + d_gmlp, 3 * d_att + d_gmlp
    out_shape = [
        jax.ShapeDtypeStruct((n_tok, d_model), F32),
        jax.ShapeDtypeStruct((n_tok, d_gu), F32),
        jax.ShapeDtypeStruct((n_tok, d_kvqg), BF16),
        jax.ShapeDtypeStruct((n_tail, d_att), F32),
        jax.ShapeDtypeStruct((n_tail, d_att), F32),
    ]
    out_specs = [rows(d_model), rows(d_gu), rows(d_kvqg), tail_spec, tail_spec]
    if with_vg_f32:
        out_shape.append(jax.ShapeDtypeStruct((n_tok, d_gmlp), F32))
        out_specs.append(rows(d_gmlp))

    weight_bytes = (3 * _nbytes((d_model, d_ff), BF16) + _nbytes((d_model, d_in), BF16)
                    + _nbytes(p['head_avg'].shape[1:], BF16))
    block_bytes = 2 * (2 * _nbytes((tm, d_model), F32) + _nbytes((tm, d_gu), F32)
                       + _nbytes((tm, d_gmlp), F32) + _nbytes((tm, d_kvqg), BF16)
                       + 2 * _nbytes((tail_rows, d_att), F32))
    scratch_bytes = _nbytes((tm, d_ff), BF16)
    temp_bytes = 6 * _nbytes((tm, d_model), F32)

    kernel = functools.partial(_token_kernel, tiles_per_seq=tiles_per_seq,
                               d_conv=d_conv, d_gmlp=d_gmlp, d_att=d_att)
    return pl.pallas_call(
        kernel,
        out_shape=out_shape,
        grid=(n_tok // tm,),
        in_specs=in_specs,
        out_specs=out_specs,
        scratch_shapes=[pltpu.VMEM((tm, d_ff), BF16)],
        compiler_params=pltpu.CompilerParams(
            dimension_semantics=("arbitrary",),
            vmem_limit_bytes=_vmem_limit(weight_bytes + block_bytes, scratch_bytes, temp_bytes)),
        name="token_kernel",
    )(x, *params, p['head_avg'])


def _mixer_kernel(x1_ref, gu_ref, gh_ref, kvqg_ref, kvh_ref,
                  cw_ref, cb_ref, clg_ref, clb_ref, wcat_ref, gmb_ref, bias_ref,
                  wout_ref, n2_ref, wg_ref, wu_ref, wd_ref,
                  y_ref,
                  gbuf, mixed, a_scr, *, first_tile_has_no_past, d_conv, d_gmlp, d_att):
    nb_count, tt, d_model = x1_ref.shape
    k_col, v_col, q_col, vg_col = 0, d_att, 2 * d_att, 3 * d_att
    gm_rows = gmb_ref.shape[0]
    t = pl.program_id(1)
    slab = V7X_MXU_COLS
    n_slabs = d_att // slab

    head_of_lane = lax.broadcasted_iota(jnp.int32, (1, slab), 1) // HEAD_DIM
    gm_head_of_lane = lax.broadcasted_iota(jnp.int32, (1, d_gmlp), 1) // GM_HEAD_DIM
    key_index = lax.broadcasted_iota(jnp.int32, (1, BAND_LEN), 1)

    for nb in range(nb_count):
        r0 = nb * tt

        hist = gh_ref[nb, :, 0:d_conv]
        if first_tile_has_no_past:
            hist = jnp.where(t > 0, hist, 0.0)
        gbuf[0:CONV_HALO, :] = hist
        gbuf[CONV_HALO:, :] = gu_ref[nb, :, 0:d_conv]
        first_tap_row = CONV_HALO - CONV_PAD
        acc = None
        for r in range(V7X_SUBLANES):
            n_rows = tt if r == 0 else tt + V7X_SUBLANES
            part = None
            for off in range(r, CONV_HALO + 1, V7X_SUBLANES):
                w = off - first_tap_row
                if 0 <= w < CONV_WIDTH:
                    term = gbuf[off - r:off - r + n_rows, :] * cw_ref[w:w + 1, :]
                    part = term if part is None else part + term
            part = part[r:r + tt, :]
            acc = part if acc is None else acc + part
        yc = _layer_norm(acc + cb_ref[...], clg_ref[...], clb_ref[...])
        mixed[r0:r0 + tt, 0:d_conv] = (yc * jax.nn.sigmoid(yc)).astype(BF16)

        for ci in range(tt // gm_rows):
            rs = ci * gm_rows
            vgc = kvqg_ref[nb, rs:rs + gm_rows, vg_col:]
            vstack = jnp.concatenate(
                [jnp.where(gm_head_of_lane == hh, vgc, jnp.zeros_like(vgc))
                 for hh in range(d_gmlp // GM_HEAD_DIM)], axis=0)
            s = _dot(wcat_ref[...], vstack) + gmb_ref[...]
            gm = gu_ref[nb, rs:rs + gm_rows, d_conv:] * s
            mixed[r0 + rs:r0 + rs + gm_rows, d_conv:d_conv + d_gmlp] = gm.astype(BF16)

        for c in range(tt // CHUNK):
            cs = c * CHUNK
            for g in range(n_slabs):
                q4 = kvqg_ref[nb, cs:cs + CHUNK, q_col + g * slab:q_col + (g + 1) * slab]
                kcols = slice(k_col + g * slab, k_col + (g + 1) * slab)
                vcols = slice(v_col + g * slab, v_col + (g + 1) * slab)
                qs = jnp.concatenate(
                    [jnp.where(head_of_lane == hl, q4, jnp.zeros_like(q4))
                     for hl in range(HEADS_PER_SLAB)], axis=0)
                if cs < BAND_PAST:
                    kb = jnp.concatenate([kvh_ref[nb, cs:, kcols], kvqg_ref[nb, 0:cs + CHUNK, kcols]], axis=0)
                    vb = jnp.concatenate([kvh_ref[nb, cs:, vcols], kvqg_ref[nb, 0:cs + CHUNK, vcols]], axis=0)
                else:
                    kb = kvqg_ref[nb, cs - BAND_PAST:cs + CHUNK, kcols]
                    vb = kvqg_ref[nb, cs - BAND_PAST:cs + CHUNK, vcols]
                s = lax.dot_general(qs, kb, (((1,), (1,)), ((), ())), preferred_element_type=F32)
                bias = bias_ref[g]
                if first_tile_has_no_past and cs < BAND_PAST:
                    first_valid = jnp.where(t == 0, BAND_PAST - cs, 0)
                    bias = jnp.where(key_index >= first_valid, bias, MASK_VALUE)
                s = s + bias
                m = jnp.max(s, axis=-1, keepdims=True)
                e = jnp.exp(s - m)
                denom = jnp.sum(e, axis=-1, keepdims=True)
                o4 = _dot(e.astype(BF16), vb) * (1.0 / denom)
                o = o4[0:CHUNK, :]
                for hl in range(1, HEADS_PER_SLAB):
                    o = jnp.where(head_of_lane == hl, o4[hl * CHUNK:(hl + 1) * CHUNK, :], o)
                col = d_conv + d_gmlp + g * slab
                mixed[r0 + cs:r0 + cs + CHUNK, col:col + slab] = o.astype(BF16)

    x1 = x1_ref[...].reshape(nb_count * tt, d_model)
    x2 = x1 + _dot(mixed[:, d_conv:], wout_ref[d_conv:, :])
    x2 = x2 + _dot(mixed[:, 0:d_conv], wout_ref[0:d_conv, :])
    y = _swiglu_half_step(x2, n2_ref, wg_ref, wu_ref, wd_ref, a_scr)
    y_ref[...] = y.reshape(nb_count, tt, d_model)


def _mixer_call(x1, gu, glu_hist, kvqg, kv_hist, p, l, tables, *, nb, tt, self_history):
    n_seq, seq_len, d_model = x1.shape
    d_conv, d_gmlp, d_att = p['d_conv'], p['d_gmlp'], p['d_att']
    d_ff = p['wg2'].shape[2]
    assert n_seq % nb == 0 and seq_len % tt == 0 and tt % CHUNK == 0
    wcat, gmb, bias_tab = tables
    gm_rows = wcat.shape[1]
    assert tt % gm_rows == 0
    n_t = seq_len // tt
    rows = nb * tt
    d_gu, d_kvqg = d_conv + d_gmlp, 3 * d_att + d_gmlp
    assert gu.shape[2] == d_gu and kvqg.shape[2] == d_kvqg

    def cur(width):
        return pl.BlockSpec((nb, tt, width), lambda b, t: (b, t, 0))

    if self_history:
        assert tt % BAND_PAST == 0 and glu_hist is None and kv_hist is None
        glu_hist, kv_hist = gu, kvqg
        gh_spec = pl.BlockSpec((nb, CONV_HALO, d_gu),
                               lambda b, t: (b, jnp.maximum(t * (tt // CONV_HALO) - 1, 0), 0))
        past_spec = pl.BlockSpec((nb, BAND_PAST, 2 * d_att),
                                 lambda b, t: (b, jnp.maximum(t * (tt // BAND_PAST) - 1, 0), 0))
    else:
        assert n_t == 1
        gh_spec = pl.BlockSpec((None, nb, CONV_HALO, d_conv), lambda b, t: (l, b, 0, 0))
        past_spec = pl.BlockSpec((None, nb, BAND_PAST, 2 * d_att), lambda b, t: (l, b, 0, 0))

    params = ([p[name] for name in ('conv_w', 'conv_b', 'conv_ln_g', 'conv_ln_b')] + [wcat, gmb, bias_tab]
              + [p[name] for name in ('w_out', 'n2', 'wg2', 'wu2', 'wd2')])
    in_specs = [cur(d_model), cur(d_gu), gh_spec, cur(d_kvqg), past_spec] + [_resident(a, l) for a in params]
    scratch = [
        pltpu.VMEM((CONV_HALO + tt, d_conv), F32),
        pltpu.VMEM((rows, d_model), BF16),
        pltpu.VMEM((rows, d_ff), BF16),
    ]
    weight_bytes = (3 * _nbytes((d_model, d_ff), BF16) + _nbytes((d_model, d_model), BF16)
                    + _nbytes(bias_tab.shape[1:], F32)
                    + _nbytes((gm_rows, 4 * gm_rows), BF16) + _nbytes((gm_rows, d_gmlp), F32))
    block_bytes = 2 * (2 * _nbytes((rows, d_model), F32) + 2 * _nbytes((rows, d_conv), F32)
                       + _nbytes((rows, d_gmlp), F32) + 3 * _nbytes((rows, d_att), BF16)
                       + 2 * _nbytes((nb, BAND_PAST, d_att), BF16) + _nbytes((nb, CONV_HALO, d_conv), F32))
    scratch_bytes = (_nbytes((CONV_HALO + tt, d_conv), F32)
                     + _nbytes((rows, d_model), BF16) + _nbytes((rows, d_ff), BF16))
    temp_bytes = 6 * _nbytes((rows, d_model), F32)

    kernel = functools.partial(_mixer_kernel, first_tile_has_no_past=self_history,
                               d_conv=d_conv, d_gmlp=d_gmlp, d_att=d_att)
    return pl.pallas_call(
        kernel,
        out_shape=jax.ShapeDtypeStruct((n_seq, seq_len, d_model), F32),
        grid=(n_seq // nb, n_t),
        in_specs=in_specs,
        out_specs=cur(d_model),
        scratch_shapes=scratch,
        compiler_params=pltpu.CompilerParams(
            dimension_semantics=("arbitrary", "arbitrary"),
            vmem_limit_bytes=_vmem_limit(weight_bytes + block_bytes, scratch_bytes, temp_bytes)),
        name="mixer_kernel",
    )(x1, gu, glu_hist, kvqg, kv_hist, *params)


def _rows(v):
    return v[:, None, :].astype(F32)


def _gmlp_tables(ws, bs, rows):
    depth, n_heads = ws.shape[:2]
    w = jnp.tril(ws[:, :, :rows, :rows])
    wcat = jnp.transpose(w, (0, 2, 1, 3)).reshape(depth, rows, n_heads * rows).astype(BF16)
    gmb = jnp.repeat(jnp.swapaxes(bs[:, :, :rows], 1, 2), GM_HEAD_DIM, axis=2).astype(F32)
    return wcat, gmb


def _band_bias(rel_bias):
    n_heads = rel_bias.shape[0]
    lo, hi = -(CHUNK - 1), BAND_LEN - 1
    assert -REL_CLIP <= lo and hi >= REL_CLIP
    by_rel = jnp.concatenate(
        [rel_bias[:, lo + REL_CLIP:], jnp.broadcast_to(rel_bias[:, -1:], (n_heads, hi - REL_CLIP))],
        axis=1).astype(F32)
    width = hi - lo + 1
    padded = jnp.pad(by_rel[:, ::-1], ((0, 0), (0, 1)))
    skew = jnp.tile(padded, (1, CHUNK))[:, :CHUNK * width].reshape(n_heads, CHUNK, width)
    tab = skew[:, :, CHUNK - 1:]
    return tab.reshape(n_heads // HEADS_PER_SLAB, HEADS_PER_SLAB * CHUNK, BAND_LEN)


def kernel(x_prompt, x_sample, cache_conv, cache_k, cache_v, ffn1_norm, ffn1_w_gate, ffn1_w_up, ffn1_w_down, mix_norm, w_in, conv_w, conv_b, conv_ln_g, conv_ln_b, gmlp_ln_g, gmlp_ln_b, gmlp_ws, gmlp_b, q_norm, k_norm, rel_bias, w_out, ffn2_norm, ffn2_w_gate, ffn2_w_up, ffn2_w_down):
    batch, seq, d_model = x_prompt.shape
    dec_batch, dec_seq, _ = x_sample.shape
    depth = w_in.shape[0]
    d_conv = conv_w.shape[2]
    d_gmlp = gmlp_ln_g.shape[1]
    n_att_heads = rel_bias.shape[1]
    d_att = n_att_heads * HEAD_DIM
    assert cache_k.shape[2] == BAND_PAST and cache_conv.shape[2] == CONV_PAD
    assert dec_seq == CHUNK and dec_seq <= GM_CHUNK

    head_avg = jnp.kron(jnp.eye(HEADS_PER_SLAB, dtype=F32),
                        jnp.full((HEAD_DIM, HEAD_DIM), 1.0 / HEAD_DIM, F32)).astype(BF16)[None]
    p = dict(
        d_conv=d_conv, d_gmlp=d_gmlp, d_att=d_att,
        n1=_rows(ffn1_norm), wg1=ffn1_w_gate.astype(BF16), wu1=ffn1_w_up.astype(BF16),
        wd1=ffn1_w_down.astype(BF16),
        nmix=_rows(mix_norm), w_in=w_in.astype(BF16),
        gm_ln_g=_rows(gmlp_ln_g), gm_ln_b=_rows(gmlp_ln_b),
        qn=_rows(jnp.tile(q_norm, (1, n_att_heads))), kn=_rows(jnp.tile(k_norm, (1, n_att_heads))),
        head_avg=head_avg,
        conv_w=conv_w.astype(F32), conv_b=_rows(conv_b),
        conv_ln_g=_rows(conv_ln_g), conv_ln_b=_rows(conv_ln_b),
        w_out=w_out.astype(BF16), n2=_rows(ffn2_norm), wg2=ffn2_w_gate.astype(BF16),
        wu2=ffn2_w_up.astype(BF16), wd2=ffn2_w_down.astype(BF16),
    )

    bias_tab = _band_bias(rel_bias.reshape(depth * n_att_heads, -1))
    bias_tab = bias_tab.reshape((depth, n_att_heads // HEADS_PER_SLAB) + bias_tab.shape[1:])
    prompt_tables = _gmlp_tables(gmlp_ws, gmlp_b, GM_CHUNK) + (bias_tab,)
    sample_tables = _gmlp_tables(gmlp_ws, gmlp_b, dec_seq) + (bias_tab,)
    conv_hist = jnp.pad(cache_conv.astype(F32), ((0, 0), (0, 0), (CONV_HALO - CONV_PAD, 0), (0, 0)))
    kv_hist = jnp.concatenate([cache_k.reshape(depth, dec_batch, BAND_PAST, d_att),
                               cache_v.reshape(depth, dec_batch, BAND_PAST, d_att)], axis=-1).astype(BF16)

    xp = x_prompt.reshape(batch * seq, d_model)
    xs = x_sample.reshape(dec_batch * dec_seq, d_model)
    outs = {name: [] for name in ('p_conv', 'p_k', 'p_v', 's_conv', 's_k', 's_v', 's_gv')}
    sample_streams_per_step = TOKEN_TILE // dec_seq // 2

    for l in range(depth):
        x1, gu, kvqg, k_tail, v_tail = _token_call(
            xp, p, l, seq_len=seq, tail_rows=min(BAND_PAST, seq), with_vg_f32=False)
        seq3 = lambda a: a.reshape(batch, seq, a.shape[-1])
        xp = _mixer_call(seq3(x1), seq3(gu), None, seq3(kvqg), None, p, l, prompt_tables,
                         nb=1, tt=TOKEN_TILE, self_history=True).reshape(batch * seq, d_model)
        outs['p_conv'].append(seq3(gu)[:, seq - CONV_PAD:, :d_conv])
        outs['p_k'].append(k_tail.reshape(batch, -1, n_att_heads, HEAD_DIM))
        outs['p_v'].append(v_tail.reshape(batch, -1, n_att_heads, HEAD_DIM))

        x1, gu, kvqg, k_new, v_new, vg = _token_call(
            xs, p, l, seq_len=dec_seq, tail_rows=TOKEN_TILE, with_vg_f32=True)
        dec3 = lambda a: a.reshape(dec_batch, dec_seq, a.shape[-1])
        xs = _mixer_call(dec3(x1), dec3(gu), conv_hist, dec3(kvqg), kv_hist, p, l, sample_tables,
                         nb=sample_streams_per_step, tt=dec_seq,
                         self_history=False).reshape(dec_batch * dec_seq, d_model)
        outs['s_conv'].append(dec3(gu)[:, dec_seq - CONV_PAD:, :d_conv])
        outs['s_k'].append(k_new.reshape(dec_batch, dec_seq, n_att_heads, HEAD_DIM))
        outs['s_v'].append(v_new.reshape(dec_batch, dec_seq, n_att_heads, HEAD_DIM))
        outs['s_gv'].append(vg.reshape(dec_batch, dec_seq, d_gmlp // GM_HEAD_DIM, GM_HEAD_DIM))

    return (xp.reshape(batch, seq, d_model), xs.reshape(dec_batch, dec_seq, d_model),
            jnp.stack(outs['p_conv']), jnp.stack(outs['p_k']), jnp.stack(outs['p_v']),
            jnp.stack(outs['s_conv']), jnp.stack(outs['s_k']), jnp.stack(outs['s_v']),
            jnp.stack(outs['s_gv']))
```

```python
import functools

import jax
import jax.numpy as jnp
from jax import lax
from jax.experimental import pallas as pl
from jax.experimental.pallas import tpu as pltpu

F32 = jnp.float32
BF16 = jnp.bfloat16

EPS = 1e-6
CHUNK = 64
BAND_CHUNKS = 8
BAND_PAST = BAND_CHUNKS * CHUNK
BAND_LEN = BAND_PAST + CHUNK
REL_CLIP = 128
CONV_WIDTH = 31
CONV_PAD = CONV_WIDTH - 1
CONV_HALO = 32
HEAD_DIM = 64
GM_HEAD_DIM = 64
GM_CHUNK = 128
MASK_VALUE = -1e30

V7X_LANES = 128
V7X_SUBLANES = 8
V7X_MXU_COLS = 256
V7X_VMEM_BYTES = 64 * 1024 * 1024
V7X_VMEM_RESERVED_BYTES = 4 * 1024 * 1024
HEADS_PER_SLAB = V7X_MXU_COLS // HEAD_DIM

TOKEN_TILE = 512
FF_CHUNK = 256


def _vmem_limit(block_bytes, scratch_bytes, temp_bytes):
    return int(min(V7X_VMEM_BYTES - V7X_VMEM_RESERVED_BYTES, block_bytes + scratch_bytes + temp_bytes))


def _nbytes(shape, dtype):
    n = 1
    for s in shape:
        n *= s
    return n * jnp.dtype(dtype).itemsize


def _rms_factors(x, g):
    return (x * g).astype(BF16), lax.rsqrt(jnp.mean(x * x, axis=-1, keepdims=True) + EPS)


def _layer_norm(x, g, b):
    mu = jnp.mean(x, axis=-1, keepdims=True)
    xc = x - mu
    var = jnp.mean(xc * xc, axis=-1, keepdims=True)
    return xc * lax.rsqrt(var + EPS) * g + b


def _dot(a, b):
    return jnp.dot(a, b, preferred_element_type=F32)


def _swiglu_half_step(x, norm_ref, wg_ref, wu_ref, wd_ref, a_scr):
    d_ff = wg_ref.shape[1]
    xg, r = _rms_factors(x, norm_ref[...])
    for c in range(0, d_ff, FF_CHUNK):
        w = min(FF_CHUNK, d_ff - c)
        g = r * _dot(xg, wg_ref[:, c:c + w])
        u = r * _dot(xg, wu_ref[:, c:c + w])
        a_scr[:, c:c + w] = (g * jax.nn.sigmoid(g) * u).astype(BF16)
    return x + 0.5 * _dot(a_scr[...], wd_ref[...])


def _token_kernel(x_ref, n1_ref, wg_ref, wu_ref, wd_ref, nmix_ref, win_ref, lng_ref, lnb_ref,
                  qn_ref, kn_ref, havg_ref,
                  x1_ref, gu_ref, kvqg_ref, ktail_ref, vtail_ref, *rest,
                  tiles_per_seq, d_conv, d_gmlp, d_att):
    *vg32_ref, a_scr = rest
    x1 = _swiglu_half_step(x_ref[...], n1_ref, wg_ref, wu_ref, wd_ref, a_scr)
    x1_ref[...] = x1
    h, r = _rms_factors(x1, nmix_ref[...])

    o1 = 2 * d_conv
    o2 = o1 + 2 * d_gmlp
    o3 = o2 + d_att
    o4 = o3 + d_att

    zc = r * _dot(h, win_ref[:, 0:o1])
    gu_ref[:, 0:d_conv] = zc[:, :d_conv] * jax.nn.sigmoid(zc[:, d_conv:])

    uv = jax.nn.gelu(r * _dot(h, win_ref[:, o1:o2]))
    gu_ref[:, d_conv:] = uv[:, :d_gmlp]
    vg = _layer_norm(uv[:, d_gmlp:], lng_ref[...], lnb_ref[...])
    kvqg_ref[:, 3 * d_att:] = vg.astype(BF16)
    if vg32_ref:
        vg32_ref[0][...] = vg

    def head_rms(z, g):
        sq = (z * z).astype(BF16)
        slab = havg_ref.shape[0]
        ms = jnp.concatenate([_dot(sq[:, c:c + slab], havg_ref[...]) for c in range(0, z.shape[1], slab)], axis=1)
        return z * lax.rsqrt(ms + EPS) * g

    zq = r * _dot(h, win_ref[:, o2:o3])
    kvqg_ref[:, 2 * d_att:3 * d_att] = head_rms(zq, qn_ref[...] * (HEAD_DIM ** -0.5)).astype(BF16)
    zk = r * _dot(h, win_ref[:, o3:o4])
    k = head_rms(zk, kn_ref[...])
    kvqg_ref[:, 0:d_att] = k.astype(BF16)
    v = r * _dot(h, win_ref[:, o4:])
    kvqg_ref[:, d_att:2 * d_att] = v.astype(BF16)

    tail_rows = ktail_ref.shape[0]

    @pl.when(pl.program_id(0) % tiles_per_seq == tiles_per_seq - 1)
    def _():
        ktail_ref[...] = k[k.shape[0] - tail_rows:, :]
        vtail_ref[...] = v[v.shape[0] - tail_rows:, :]


def _resident(stack, l):
    shape = stack.shape[1:]
    return pl.BlockSpec((None,) + shape, lambda *_: (l,) + (0,) * len(shape), pipeline_mode=pl.Buffered(1))


def _token_call(x, p, l, *, seq_len, tail_rows, with_vg_f32):
    n_tok, d_model = x.shape
    tm = TOKEN_TILE
    assert n_tok % tm == 0 and (seq_len % tm == 0 or tm % seq_len == 0)
    tiles_per_seq = max(seq_len // tm, 1)
    assert tail_rows <= tm
    n_tail = (n_tok // tm // tiles_per_seq) * tail_rows
    d_ff = p['wg1'].shape[2]
    d_in = p['w_in'].shape[2]
    d_conv, d_gmlp, d_att = p['d_conv'], p['d_gmlp'], p['d_att']

    def rows(width):
        return pl.BlockSpec((tm, width), lambda i: (i, 0))

    tail_spec = pl.BlockSpec((tail_rows, d_att), lambda i: (i // tiles_per_seq, 0))
    params = [p[name] for name in ('n1', 'wg1', 'wu1', 'wd1', 'nmix', 'w_in', 'gm_ln_g', 'gm_ln_b', 'qn', 'kn')]
    in_specs = [rows(d_model)] + [_resident(a, l) for a in params] + [_resident(p['head_avg'], 0)]
    d_gu, d_kvqg = d_conv + d_gmlp, 3 * d_att + d_gmlp
    out_shape = [
        jax.ShapeDtypeStruct((n_tok, d_model), F32),
        jax.ShapeDtypeStruct((n_tok, d_gu), F32),
        jax.ShapeDtypeStruct((n_tok, d_kvqg), BF16),
        jax.ShapeDtypeStruct((n_tail, d_att), F32),
        jax.ShapeDtypeStruct((n_tail, d_att), F32),
    ]
    out_specs = [rows(d_model), rows(d_gu), rows(d_kvqg), tail_spec, tail_spec]
    if with_vg_f32:
        out_shape.append(jax.ShapeDtypeStruct((n_tok, d_gmlp), F32))
        out_specs.append(rows(d_gmlp))

    weight_bytes = (3 * _nbytes((d_model, d_ff), BF16) + _nbytes((d_model, d_in), BF16)
                    + _nbytes(p['head_avg'].shape[1:], BF16))
    block_bytes = 2 * (2 * _nbytes((tm, d_model), F32) + _nbytes((tm, d_gu), F32)
                       + _nbytes((tm, d_gmlp), F32) + _nbytes((tm, d_kvqg), BF16)
                       + 2 * _nbytes((tail_rows, d_att), F32))
    scratch_bytes = _nbytes((tm, d_ff), BF16)
    temp_bytes = 6 * _nbytes((tm, d_model), F32)

    kernel = functools.partial(_token_kernel, tiles_per_seq=tiles_per_seq,
                               d_conv=d_conv, d_gmlp=d_gmlp, d_att=d_att)
    return pl.pallas_call(
        kernel,
        out_shape=out_shape,
        grid=(n_tok // tm,),
        in_specs=in_specs,
        out_specs=out_specs,
        scratch_shapes=[pltpu.VMEM((tm, d_ff), BF16)],
        compiler_params=pltpu.CompilerParams(
            dimension_semantics=("arbitrary",),
            vmem_limit_bytes=_vmem_limit(weight_bytes + block_bytes, scratch_bytes, temp_bytes)),
        name="token_kernel",
    )(x, *params, p['head_avg'])


def _mixer_kernel(x1_ref, gu_ref, gh_ref, kvqg_ref, kvh_ref,
                  cw_ref, cb_ref, clg_ref, clb_ref, wcat_ref, gmb_ref, bias_ref,
                  wout_ref, n2_ref, wg_ref, wu_ref, wd_ref,
                  y_ref,
                  gbuf, mixed, a_scr, *, first_tile_has_no_past, d_conv, d_gmlp, d_att):
    nb_count, tt, d_model = x1_ref.shape
    k_col, v_col, q_col, vg_col = 0, d_att, 2 * d_att, 3 * d_att
    gm_rows = gmb_ref.shape[0]
    t = pl.program_id(1)
    slab = V7X_MXU_COLS
    n_slabs = d_att // slab

    head_of_lane = lax.broadcasted_iota(jnp.int32, (1, slab), 1) // HEAD_DIM
    gm_head_of_lane = lax.broadcasted_iota(jnp.int32, (1, d_gmlp), 1) // GM_HEAD_DIM
    key_index = lax.broadcasted_iota(jnp.int32, (1, BAND_LEN), 1)

    for nb in range(nb_count):
        r0 = nb * tt

        hist = gh_ref[nb, :, 0:d_conv]
        if first_tile_has_no_past:
            hist = jnp.where(t > 0, hist, 0.0)
        gbuf[0:CONV_HALO, :] = hist
        gbuf[CONV_HALO:, :] = gu_ref[nb, :, 0:d_conv]
        first_tap_row = CONV_HALO - CONV_PAD
        lane_tiles = []
        for c0 in range(0, d_conv, V7X_LANES):
            ch = slice(c0, c0 + V7X_LANES)
            acc = None
            for r in range(V7X_SUBLANES):
                n_rows = tt if r == 0 else tt + V7X_SUBLANES
                part = None
                for off in range(r, CONV_HALO + 1, V7X_SUBLANES):
                    w = off - first_tap_row
                    if 0 <= w < CONV_WIDTH:
                        term = gbuf[off - r:off - r + n_rows, ch] * cw_ref[w:w + 1, ch]
                        part = term if part is None else part + term
                part = part[r:r + tt, :]
                acc = part if acc is None else acc + part
            lane_tiles.append(acc)
        acc = jnp.concatenate(lane_tiles, axis=1)
        yc = _layer_norm(acc + cb_ref[...], clg_ref[...], clb_ref[...])
        mixed[r0:r0 + tt, 0:d_conv] = (yc * jax.nn.sigmoid(yc)).astype(BF16)

        for ci in range(tt // gm_rows):
            rs = ci * gm_rows
            vgc = kvqg_ref[nb, rs:rs + gm_rows, vg_col:]
            vstack = jnp.concatenate(
                [jnp.where(gm_head_of_lane == hh, vgc, jnp.zeros_like(vgc))
                 for hh in range(d_gmlp // GM_HEAD_DIM)], axis=0)
            s = _dot(wcat_ref[...], vstack) + gmb_ref[...]
            gm = gu_ref[nb, rs:rs + gm_rows, d_conv:] * s
            mixed[r0 + rs:r0 + rs + gm_rows, d_conv:d_conv + d_gmlp] = gm.astype(BF16)

        for c in range(tt // CHUNK):
            cs = c * CHUNK
            for g in range(n_slabs):
                q4 = kvqg_ref[nb, cs:cs + CHUNK, q_col + g * slab:q_col + (g + 1) * slab]
                kcols = slice(k_col + g * slab, k_col + (g + 1) * slab)
                vcols = slice(v_col + g * slab, v_col + (g + 1) * slab)
                qs = jnp.concatenate(
                    [jnp.where(head_of_lane == hl, q4, jnp.zeros_like(q4))
                     for hl in range(HEADS_PER_SLAB)], axis=0)
                if cs < BAND_PAST:
                    kb = jnp.concatenate([kvh_ref[nb, cs:, kcols], kvqg_ref[nb, 0:cs + CHUNK, kcols]], axis=0)
                    vb = jnp.concatenate([kvh_ref[nb, cs:, vcols], kvqg_ref[nb, 0:cs + CHUNK, vcols]], axis=0)
                else:
                    kb = kvqg_ref[nb, cs - BAND_PAST:cs + CHUNK, kcols]
                    vb = kvqg_ref[nb, cs - BAND_PAST:cs + CHUNK, vcols]
                s = lax.dot_general(qs, kb, (((1,), (1,)), ((), ())), preferred_element_type=F32)
                bias = bias_ref[g]
                if first_tile_has_no_past and cs < BAND_PAST:
                    first_valid = jnp.where(t == 0, BAND_PAST - cs, 0)
                    bias = jnp.where(key_index >= first_valid, bias, MASK_VALUE)
                s = s + bias
                m = jnp.max(s, axis=-1, keepdims=True)
                e = jnp.exp(s - m)
                denom = jnp.sum(e, axis=-1, keepdims=True)
                o4 = _dot(e.astype(BF16), vb) * (1.0 / denom)
                o = o4[0:CHUNK, :]
                for hl in range(1, HEADS_PER_SLAB):
                    o = jnp.where(head_of_lane == hl, o4[hl * CHUNK:(hl + 1) * CHUNK, :], o)
                col = d_conv + d_gmlp + g * slab
                mixed[r0 + cs:r0 + cs + CHUNK, col:col + slab] = o.astype(BF16)

    x1 = x1_ref[...].reshape(nb_count * tt, d_model)
    x2 = x1 + _dot(mixed[:, d_conv:], wout_ref[d_conv:, :])
    x2 = x2 + _dot(mixed[:, 0:d_conv], wout_ref[0:d_conv, :])
    y = _swiglu_half_step(x2, n2_ref, wg_ref, wu_ref, wd_ref, a_scr)
    y_ref[...] = y.reshape(nb_count, tt, d_model)


def _mixer_call(x1, gu, glu_hist, kvqg, kv_hist, p, l, tables, *, nb, tt, self_history):
    n_seq, seq_len, d_model = x1.shape
    d_conv, d_gmlp, d_att = p['d_conv'], p['d_gmlp'], p['d_att']
    d_ff = p['wg2'].shape[2]
    assert n_seq % nb == 0 and seq_len % tt == 0 and tt % CHUNK == 0
    wcat, gmb, bias_tab = tables
    gm_rows = wcat.shape[1]
    assert tt % gm_rows == 0
    n_t = seq_len // tt
    rows = nb * tt
    d_gu, d_kvqg = d_conv + d_gmlp, 3 * d_att + d_gmlp
    assert gu.shape[2] == d_gu and kvqg.shape[2] == d_kvqg

    def cur(width):
        return pl.BlockSpec((nb, tt, width), lambda b, t: (b, t, 0))

    if self_history:
        assert tt % BAND_PAST == 0 and glu_hist is None and kv_hist is None
        glu_hist, kv_hist = gu, kvqg
        gh_spec = pl.BlockSpec((nb, CONV_HALO, d_gu),
                               lambda b, t: (b, jnp.maximum(t * (tt // CONV_HALO) - 1, 0), 0))
        past_spec = pl.BlockSpec((nb, BAND_PAST, 2 * d_att),
                                 lambda b, t: (b, jnp.maximum(t * (tt // BAND_PAST) - 1, 0), 0))
    else:
        assert n_t == 1
        gh_spec = pl.BlockSpec((None, nb, CONV_HALO, d_conv), lambda b, t: (l, b, 0, 0))
        past_spec = pl.BlockSpec((None, nb, BAND_PAST, 2 * d_att), lambda b, t: (l, b, 0, 0))

    params = ([p[name] for name in ('conv_w', 'conv_b', 'conv_ln_g', 'conv_ln_b')] + [wcat, gmb, bias_tab]
              + [p[name] for name in ('w_out', 'n2', 'wg2', 'wu2', 'wd2')])
    in_specs = [cur(d_model), cur(d_gu), gh_spec, cur(d_kvqg), past_spec] + [_resident(a, l) for a in params]
    scratch = [
        pltpu.VMEM((CONV_HALO + tt, d_conv), F32),
        pltpu.VMEM((rows, d_model), BF16),
        pltpu.VMEM((rows, d_ff), BF16),
    ]
    weight_bytes = (3 * _nbytes((d_model, d_ff), BF16) + _nbytes((d_model, d_model), BF16)
                    + _nbytes(bias_tab.shape[1:], F32)
                    + _nbytes((gm_rows, 4 * gm_rows), BF16) + _nbytes((gm_rows, d_gmlp), F32))
    block_bytes = 2 * (2 * _nbytes((rows, d_model), F32) + 2 * _nbytes((rows, d_conv), F32)
                       + _nbytes((rows, d_gmlp), F32) + 3 * _nbytes((rows, d_att), BF16)
                       + 2 * _nbytes((nb, BAND_PAST, d_att), BF16) + _nbytes((nb, CONV_HALO, d_conv), F32))
    scratch_bytes = (_nbytes((CONV_HALO + tt, d_conv), F32)
                     + _nbytes((rows, d_model), BF16) + _nbytes((rows, d_ff), BF16))
    temp_bytes = 6 * _nbytes((rows, d_model), F32)

    kernel = functools.partial(_mixer_kernel, first_tile_has_no_past=self_history,
                               d_conv=d_conv, d_gmlp=d_gmlp, d_att=d_att)
    return pl.pallas_call(
        kernel,
        out_shape=jax.ShapeDtypeStruct((n_seq, seq_len, d_model), F32),
        grid=(n_seq // nb, n_t),
        in_specs=in_specs,
        out_specs=cur(d_model),
        scratch_shapes=scratch,
        compiler_params=pltpu.CompilerParams(
            dimension_semantics=("arbitrary", "arbitrary"),
            vmem_limit_bytes=_vmem_limit(weight_bytes + block_bytes, scratch_bytes, temp_bytes)),
        name="mixer_kernel",
    )(x1, gu, glu_hist, kvqg, kv_hist, *params)


def _rows(v):
    return v[:, None, :].astype(F32)


def _gmlp_tables(ws, bs, rows):
    depth, n_heads = ws.shape[:2]
    w = jnp.tril(ws[:, :, :rows, :rows])
    wcat = jnp.transpose(w, (0, 2, 1, 3)).reshape(depth, rows, n_heads * rows).astype(BF16)
    gmb = jnp.repeat(jnp.swapaxes(bs[:, :, :rows], 1, 2), GM_HEAD_DIM, axis=2).astype(F32)
    return wcat, gmb


def _band_bias(rel_bias):
    n_heads = rel_bias.shape[0]
    lo, hi = -(CHUNK - 1), BAND_LEN - 1
    assert -REL_CLIP <= lo and hi >= REL_CLIP
    by_rel = jnp.concatenate(
        [rel_bias[:, lo + REL_CLIP:], jnp.broadcast_to(rel_bias[:, -1:], (n_heads, hi - REL_CLIP))],
        axis=1).astype(F32)
    width = hi - lo + 1
    padded = jnp.pad(by_rel[:, ::-1], ((0, 0), (0, 1)))
    skew = jnp.tile(padded, (1, CHUNK))[:, :CHUNK * width].reshape(n_heads, CHUNK, width)
    tab = skew[:, :, CHUNK - 1:]
    return tab.reshape(n_heads // HEADS_PER_SLAB, HEADS_PER_SLAB * CHUNK, BAND_LEN)


def kernel(x_prompt, x_sample, cache_conv, cache_k, cache_v, ffn1_norm, ffn1_w_gate, ffn1_w_up, ffn1_w_down, mix_norm, w_in, conv_w, conv_b, conv_ln_g, conv_ln_b, gmlp_ln_g, gmlp_ln_b, gmlp_ws, gmlp_b, q_norm, k_norm, rel_bias, w_out, ffn2_norm, ffn2_w_gate, ffn2_w_up, ffn2_w_down):
    batch, seq, d_model = x_prompt.shape
    dec_batch, dec_seq, _ = x_sample.shape
    depth = w_in.shape[0]
    d_conv = conv_w.shape[2]
    d_gmlp = gmlp_ln_g.shape[1]
    n_att_heads = rel_bias.shape[1]
    d_att = n_att_heads * HEAD_DIM
    assert cache_k.shape[2] == BAND_PAST and cache_conv.shape[2] == CONV_PAD
    assert dec_seq == CHUNK and dec_seq <= GM_CHUNK

    head_avg = jnp.kron(jnp.eye(HEADS_PER_SLAB, dtype=F32),
                        jnp.full((HEAD_DIM, HEAD_DIM), 1.0 / HEAD_DIM, F32)).astype(BF16)[None]
    p = dict(
        d_conv=d_conv, d_gmlp=d_gmlp, d_att=d_att,
        n1=_rows(ffn1_norm), wg1=ffn1_w_gate.astype(BF16), wu1=ffn1_w_up.astype(BF16),
        wd1=ffn1_w_down.astype(BF16),
        nmix=_rows(mix_norm), w_in=w_in.astype(BF16),
        gm_ln_g=_rows(gmlp_ln_g), gm_ln_b=_rows(gmlp_ln_b),
        qn=_rows(jnp.tile(q_norm, (1, n_att_heads))), kn=_rows(jnp.tile(k_norm, (1, n_att_heads))),
        head_avg=head_avg,
        conv_w=conv_w.astype(F32), conv_b=_rows(conv_b),
        conv_ln_g=_rows(conv_ln_g), conv_ln_b=_rows(conv_ln_b),
        w_out=w_out.astype(BF16), n2=_rows(ffn2_norm), wg2=ffn2_w_gate.astype(BF16),
        wu2=ffn2_w_up.astype(BF16), wd2=ffn2_w_down.astype(BF16),
    )

    bias_tab = _band_bias(rel_bias.reshape(depth * n_att_heads, -1))
    bias_tab = bias_tab.reshape((depth, n_att_heads // HEADS_PER_SLAB) + bias_tab.shape[1:])
    prompt_tables = _gmlp_tables(gmlp_ws, gmlp_b, GM_CHUNK) + (bias_tab,)
    sample_tables = _gmlp_tables(gmlp_ws, gmlp_b, dec_seq) + (bias_tab,)
    conv_hist = jnp.pad(cache_conv.astype(F32), ((0, 0), (0, 0), (CONV_HALO - CONV_PAD, 0), (0, 0)))
    kv_hist = jnp.concatenate([cache_k.reshape(depth, dec_batch, BAND_PAST, d_att),
                               cache_v.reshape(depth, dec_batch, BAND_PAST, d_att)], axis=-1).astype(BF16)

    xp = x_prompt.reshape(batch * seq, d_model)
    xs = x_sample.reshape(dec_batch * dec_seq, d_model)
    outs = {name: [] for name in ('p_conv', 'p_k', 'p_v', 's_conv', 's_k', 's_v', 's_gv')}
    sample_streams_per_step = TOKEN_TILE // dec_seq // 2

    for l in range(depth):
        x1, gu, kvqg, k_tail, v_tail = _token_call(
            xp, p, l, seq_len=seq, tail_rows=min(BAND_PAST, seq), with_vg_f32=False)
        seq3 = lambda a: a.reshape(batch, seq, a.shape[-1])
        xp = _mixer_call(seq3(x1), seq3(gu), None, seq3(kvqg), None, p, l, prompt_tables,
                         nb=1, tt=TOKEN_TILE, self_history=True).reshape(batch * seq, d_model)
        outs['p_conv'].append(seq3(gu)[:, seq - CONV_PAD:, :d_conv])
        outs['p_k'].append(k_tail.reshape(batch, -1, n_att_heads, HEAD_DIM))
        outs['p_v'].append(v_tail.reshape(batch, -1, n_att_heads, HEAD_DIM))

        x1, gu, kvqg, k_new, v_new, vg = _token_call(
            xs, p, l, seq_len=dec_seq, tail_rows=TOKEN_TILE, with_vg_f32=True)
        dec3 = lambda a: a.reshape(dec_batch, dec_seq, a.shape[-1])
        xs = _mixer_call(dec3(x1), dec3(gu), conv_hist, dec3(kvqg), kv_hist, p, l, sample_tables,
                         nb=sample_streams_per_step, tt=dec_seq,
                         self_history=False).reshape(dec_batch * dec_seq, d_model)
        outs['s_conv'].append(dec3(gu)[:, dec_seq - CONV_PAD:, :d_conv])
        outs['s_k'].append(k_new.reshape(dec_batch, dec_seq, n_att_heads, HEAD_DIM))
        outs['s_v'].append(v_new.reshape(dec_batch, dec_seq, n_att_heads, HEAD_DIM))
        outs['s_gv'].append(vg.reshape(dec_batch, dec_seq, d_gmlp // GM_HEAD_DIM, GM_HEAD_DIM))

    return (xp.reshape(batch, seq, d_model), xs.reshape(dec_batch, dec_seq, d_model),
            jnp.stack(outs['p_conv']), jnp.stack(outs['p_k']), jnp.stack(outs['p_v']),
            jnp.stack(outs['s_conv']), jnp.stack(outs['s_k']), jnp.stack(outs['s_v']),
            jnp.stack(outs['s_gv']))
```
